```python
import jax, jax.numpy as jnp
from jax import lax
import numpy as np

D_MODEL = 1024
BATCH = 16
SEQ = 2048
DEPTH = 1

D_FF = 2816
W_A = D_MODEL // 2
H_A = 8
HD_A = W_A // H_A
W_B = D_MODEL - W_A
G_B = 8
CHUNK = 128
CONV_K = 31
N_MOD = 9
EPS = 1e-6
HALF = 0.5

kernel_name = "hybrid_gmlp_conformer_macaron_adaln"


def _rms_norm(x, g):
    xf = x.astype(jnp.float32)
    xf = xf * lax.rsqrt(jnp.mean(xf * xf, axis=-1, keepdims=True) + EPS)
    return (xf * g.astype(jnp.float32)).astype(x.dtype)


def _layer_norm(x, g, b):
    xf = x.astype(jnp.float32)
    mu = jnp.mean(xf, axis=-1, keepdims=True)
    xc = xf - mu
    var = jnp.mean(xc * xc, axis=-1, keepdims=True)
    y = xc * lax.rsqrt(var + EPS) * g.astype(jnp.float32) + b.astype(jnp.float32)
    return y.astype(x.dtype)


def _modulate(h, shift, scale):
    return h * (1 + scale[:, None, :]) + shift[:, None, :]


def _swiglu(h, w_in, w_out):
    gate, up = jnp.split(h @ w_in, 2, axis=-1)
    return (jax.nn.silu(gate) * up) @ w_out


def _hybrid_mixer(h, w_mix_in, gmlp_norm_g, gmlp_norm_b, w_spatial, b_spatial,
                  conv_w, conv_b, conv_norm_g, conv_norm_b, g_out_a, g_out_b, w_mix_out):
    bsz, seq, _ = h.shape
    proj = h @ w_mix_in
    u, v, a, g = jnp.split(proj, [W_A, 2 * W_A, 2 * W_A + W_B], axis=-1)

    v = _layer_norm(v, gmlp_norm_g, gmlp_norm_b)
    n_chunks = seq // CHUNK
    v = v.reshape(bsz, n_chunks, CHUNK, H_A, HD_A)
    causal = jnp.tril(jnp.ones((CHUNK, CHUNK), dtype=bool))
    w_s = jnp.where(causal[None], w_spatial, jnp.zeros_like(w_spatial))
    z = jnp.einsum('hts,bnshd->bnthd', w_s, v) + b_spatial.T[None, None, :, :, None]
    y_a = u * z.reshape(bsz, seq, W_A)

    glu = a * jax.nn.sigmoid(g)
    conv = lax.conv_general_dilated(
        glu, conv_w[:, None, :], window_strides=(1,), padding=[(CONV_K - 1, 0)],
        dimension_numbers=('NWC', 'WIO', 'NWC'), feature_group_count=W_B) + conv_b
    y_b = jax.nn.silu(_layer_norm(conv, conv_norm_g, conv_norm_b))

    y = jnp.concatenate([_rms_norm(y_a, g_out_a), _rms_norm(y_b, g_out_b)], axis=-1)
    return y @ w_mix_out


def setup_inputs(seed: int = 0) -> dict:
    key = jax.random.key(seed)
    ks = jax.random.split(key, 32)
    L, D = DEPTH, D_MODEL

    def nrm(k, shape, std):
        return std * jax.random.normal(k, shape, jnp.float32)

    def gain(k, shape):
        return 1.0 + 0.05 * jax.random.normal(k, shape, jnp.float32)

    return {
        "x": nrm(ks[0], (BATCH, SEQ, D), 1.0),
        "c": nrm(ks[1], (BATCH, D), 1.0),
        "w_ada": nrm(ks[2], (L, D, N_MOD * D), 0.5 * D ** -0.5),
        "b_ada": nrm(ks[3], (L, N_MOD * D), 0.02),
        "g_pre_f1": gain(ks[4], (L, D)),
        "g_post_f1": gain(ks[5], (L, D)),
        "w_f1_in": nrm(ks[6], (L, D, 2 * D_FF), D ** -0.5),
        "w_f1_out": nrm(ks[7], (L, D_FF, D), D_FF ** -0.5),
        "g_pre_m": gain(ks[8], (L, D)),
        "g_post_m": gain(ks[9], (L, D)),
        "w_mix_in": nrm(ks[10], (L, D, 2 * W_A + 2 * W_B), D ** -0.5),
        "gmlp_norm_g": gain(ks[11], (L, W_A)),
        "gmlp_norm_b": nrm(ks[12], (L, W_A), 0.02),
        "w_spatial": nrm(ks[13], (L, H_A, CHUNK, CHUNK), CHUNK ** -0.5),
        "b_spatial": gain(ks[14], (L, H_A, CHUNK)),
        "conv_w": nrm(ks[15], (L, CONV_K, W_B), CONV_K ** -0.5),
        "conv_b": nrm(ks[16], (L, W_B), 0.02),
        "conv_norm_g": gain(ks[17], (L, W_B)),
        "conv_norm_b": nrm(ks[18], (L, W_B), 0.02),
        "g_out_a": gain(ks[19], (L, W_A)),
        "g_out_b": gain(ks[20], (L, W_B)),
        "w_mix_out": nrm(ks[21], (L, W_A + W_B, D), (W_A + W_B) ** -0.5),
        "g_pre_f2": gain(ks[22], (L, D)),
        "g_post_f2": gain(ks[23], (L, D)),
        "w_f2_in": nrm(ks[24], (L, D, 2 * D_FF), D ** -0.5),
        "w_f2_out": nrm(ks[25], (L, D_FF, D), D_FF ** -0.5),
    }


def reference(x, c, w_ada, b_ada, g_pre_f1, g_post_f1, w_f1_in, w_f1_out,
              g_pre_m, g_post_m, w_mix_in, gmlp_norm_g, gmlp_norm_b, w_spatial, b_spatial,
              conv_w, conv_b, conv_norm_g, conv_norm_b, g_out_a, g_out_b, w_mix_out,
              g_pre_f2, g_post_f2, w_f2_in, w_f2_out):
    for l in range(DEPTH):
        ada = jax.nn.silu(c) @ w_ada[l] + b_ada[l]
        sh1, sc1, gt1, sh2, sc2, gt2, sh3, sc3, gt3 = jnp.split(ada, N_MOD, axis=-1)

        h = _modulate(_rms_norm(x, g_pre_f1[l]), sh1, sc1)
        x = x + HALF * gt1[:, None, :] * _rms_norm(_swiglu(h, w_f1_in[l], w_f1_out[l]), g_post_f1[l])

        h = _modulate(_rms_norm(x, g_pre_m[l]), sh2, sc2)
        y = _hybrid_mixer(h, w_mix_in[l], gmlp_norm_g[l], gmlp_norm_b[l], w_spatial[l], b_spatial[l],
                          conv_w[l], conv_b[l], conv_norm_g[l], conv_norm_b[l],
                          g_out_a[l], g_out_b[l], w_mix_out[l])
        x = x + gt2[:, None, :] * _rms_norm(y, g_post_m[l])

        h = _modulate(_rms_norm(x, g_pre_f2[l]), sh3, sc3)
        x = x + HALF * gt3[:, None, :] * _rms_norm(_swiglu(h, w_f2_in[l], w_f2_out[l]), g_post_f2[l])
    return x
```

```python
import functools

import jax
import jax.numpy as jnp
from jax import lax
from jax.experimental import pallas as pl
from jax.experimental.pallas import tpu as pltpu

N_MOD = 9
H_A = 8
CHUNK = 128
CONV_K = 31
EPS = 1e-6
HALF = 0.5

CONV_HALO = 32
CONV_ROWS = 32
LANES = 128

FFN_TM = 512
MIX_TM = 512
ADA_TN = 1024
VMEM_LIMIT = 56 * 1024 * 1024

_F32 = jnp.float32
_BF16 = jnp.bfloat16


def _rms(x, g):
    return x * lax.rsqrt(jnp.mean(x * x, axis=-1, keepdims=True) + EPS) * g


def _layer_norm(x, g, b):
    mu = jnp.mean(x, axis=-1, keepdims=True)
    xc = x - mu
    var = jnp.mean(xc * xc, axis=-1, keepdims=True)
    return xc * lax.rsqrt(var + EPS) * g + b


def _silu(x):
    return x * jax.nn.sigmoid(x)


def _dot(a, b):
    return jnp.dot(a, b, preferred_element_type=_F32)


def _ada_kernel(c_ref, w_ref, b_ref, o_ref):
    s = _silu(c_ref[...]).astype(_BF16)
    o_ref[...] = _dot(s, w_ref[...].astype(_BF16)) + b_ref[...]


def _ada(c, w, b):
    bsz, d = c.shape
    n = w.shape[1]
    return pl.pallas_call(
        _ada_kernel,
        grid=(n // ADA_TN,),
        in_specs=[
            pl.BlockSpec((bsz, d), lambda j: (0, 0)),
            pl.BlockSpec((d, ADA_TN), lambda j: (0, j)),
            pl.BlockSpec((1, ADA_TN), lambda j: (0, j)),
        ],
        out_specs=pl.BlockSpec((bsz, ADA_TN), lambda j: (0, j)),
        out_shape=jax.ShapeDtypeStruct((bsz, n), _F32),
        compiler_params=pltpu.CompilerParams(
            dimension_semantics=("parallel",), vmem_limit_bytes=VMEM_LIMIT),
        name="ada_proj",
    )(c, w, b.reshape(1, n))


def _ffn_kernel(x_ref, ada_ref, gpre_ref, gpost_ref, win_ref, wout_ref, o_ref, *, mod_row, d_ff):
    x = x_ref[...]
    ada = ada_ref[0]
    shift = ada[mod_row:mod_row + 1]
    scale = ada[mod_row + 1:mod_row + 2]
    gate = ada[mod_row + 2:mod_row + 3]
    h = (_rms(x, gpre_ref[...]) * (1 + scale) + shift).astype(_BF16)
    g = _dot(h, win_ref[:, :d_ff])
    u = _dot(h, win_ref[:, d_ff:])
    act = (_silu(g) * u).astype(_BF16)
    y = _dot(act, wout_ref[...])
    o_ref[...] = x + HALF * gate * _rms(y, gpost_ref[...])


def _resident(shape):
    return pl.BlockSpec(shape, lambda i: (0,) * len(shape), pipeline_mode=pl.Buffered(1))


def _ffn(x2, ada3, g_pre, g_post, w_in, w_out, *, mod_row, seq):
    m, d = x2.shape
    d_ff = w_out.shape[0]
    tm = FFN_TM
    tiles_per_seq = seq // tm
    return pl.pallas_call(
        functools.partial(_ffn_kernel, mod_row=mod_row, d_ff=d_ff),
        grid=(m // tm,),
        in_specs=[
            pl.BlockSpec((tm, d), lambda i: (i, 0)),
            pl.BlockSpec((1, N_MOD, d), lambda i: (i // tiles_per_seq, 0, 0)),
            _resident((1, d)),
            _resident((1, d)),
            _resident((d, 2 * d_ff)),
            _resident((d_ff, d)),
        ],
        out_specs=pl.BlockSpec((tm, d), lambda i: (i, 0)),
        out_shape=jax.ShapeDtypeStruct((m, d), _F32),
        compiler_params=pltpu.CompilerParams(
            dimension_semantics=("parallel",), vmem_limit_bytes=VMEM_LIMIT),
        name=f"ffn_mod{mod_row}",
    )(x2, ada3, g_pre.reshape(1, d), g_post.reshape(1, d), w_in, w_out)


def _mixer_kernel(x_ref, ada_ref, gpre_ref, gpost_ref, wmi_ref, lng_ref, lnb_ref, wsp_ref, bsp_ref,
                  cw_ref, cb_ref, cng_ref, cnb_ref, goa_ref, gob_ref, wmo_ref, o_ref, gbuf,
                  *, tiles_per_seq):
    tm, d = x_ref.shape
    w_a = lng_ref.shape[1]
    w_b = cw_ref.shape[1]
    n_pairs = w_a // LANES

    @pl.when(pl.program_id(0) % tiles_per_seq == 0)
    def _():
        gbuf[0:CONV_HALO, :] = jnp.zeros((CONV_HALO, w_b), _F32)

    x = x_ref[...]
    ada = ada_ref[0]
    shift, scale, gate = ada[3:4], ada[4:5], ada[5:6]
    h = (_rms(x, gpre_ref[...]) * (1 + scale) + shift).astype(_BF16)
    proj = _dot(h, wmi_ref[...])
    u = proj[:, :w_a]
    v = proj[:, w_a:2 * w_a]
    a = proj[:, 2 * w_a:2 * w_a + w_b]
    g = proj[:, 2 * w_a + w_b:]

    v = _layer_norm(v, lng_ref[...], lnb_ref[...])
    t_idx = lax.broadcasted_iota(jnp.int32, (CHUNK, 2 * CHUNK), 0)
    s_idx = lax.broadcasted_iota(jnp.int32, (CHUNK, 2 * CHUNK), 1) % CHUNK
    lane = lax.broadcasted_iota(jnp.int32, (CHUNK, LANES), 1)
    left = lane < (LANES // 2)
    w_pairs = [jnp.where(s_idx <= t_idx, wsp_ref[p], 0.0).astype(_BF16) for p in range(n_pairs)]
    z_rows = []
    for n in range(tm // CHUNK):
        z_cols = []
        for p in range(n_pairs):
            vp = v[n * CHUNK:(n + 1) * CHUNK, p * LANES:(p + 1) * LANES]
            rhs = jnp.concatenate([jnp.where(left, vp, 0.0), jnp.where(left, 0.0, vp)], axis=0)
            z_cols.append(_dot(w_pairs[p], rhs.astype(_BF16)))
        z_rows.append(jnp.concatenate(z_cols, axis=1) + bsp_ref[...])
    y_a = _rms(u * jnp.concatenate(z_rows, axis=0), goa_ref[...])

    gbuf[CONV_HALO:CONV_HALO + tm, :] = a * jax.nn.sigmoid(g)
    first_tap = CONV_HALO - (CONV_K - 1)
    conv_rows = []
    for r in range(tm // CONV_ROWS):
        base = r * CONV_ROWS + first_tap
        acc = gbuf[base:base + CONV_ROWS, :] * cw_ref[0:1, :]
        for k in range(1, CONV_K):
            acc = acc + gbuf[base + k:base + k + CONV_ROWS, :] * cw_ref[k:k + 1, :]
        conv_rows.append(acc)
    conv = jnp.concatenate(conv_rows, axis=0) + cb_ref[...]
    gbuf[0:CONV_HALO, :] = gbuf[tm:tm + CONV_HALO, :]
    y_b = _rms(_silu(_layer_norm(conv, cng_ref[...], cnb_ref[...])), gob_ref[...])

    y = jnp.concatenate([y_a, y_b], axis=1).astype(_BF16)
    o_ref[...] = x + gate * _rms(_dot(y, wmo_ref[...]), gpost_ref[...])


def _mixer(x2, ada3, g_pre, g_post, w_mix_in, lng, lnb, w_sp_pairs, b_sp_full, conv_w, conv_b,
           cng, cnb, goa, gob, w_mix_out, *, seq):
    m, d = x2.shape
    w_a = lng.shape[0]
    w_b = conv_w.shape[1]
    tm = MIX_TM
    tiles_per_seq = seq // tm
    row = lambda v: v.reshape(1, -1)
    return pl.pallas_call(
        functools.partial(_mixer_kernel, tiles_per_seq=tiles_per_seq),
        grid=(m // tm,),
        in_specs=[
            pl.BlockSpec((tm, d), lambda i: (i, 0)),
            pl.BlockSpec((1, N_MOD, d), lambda i: (i // tiles_per_seq, 0, 0)),
            _resident((1, d)),
            _resident((1, d)),
            _resident(w_mix_in.shape),
            _resident((1, w_a)),
            _resident((1, w_a)),
            _resident(w_sp_pairs.shape),
            _resident(b_sp_full.shape),
            _resident(conv_w.shape),
            _resident((1, w_b)),
            _resident((1, w_b)),
            _resident((1, w_b)),
            _resident((1, w_a)),
            _resident((1, w_b)),
            _resident(w_mix_out.shape),
        ],
        out_specs=pl.BlockSpec((tm, d), lambda i: (i, 0)),
        out_shape=jax.ShapeDtypeStruct((m, d), _F32),
        scratch_shapes=[pltpu.VMEM((CONV_HALO + tm, w_b), _F32)],
        compiler_params=pltpu.CompilerParams(
            dimension_semantics=("arbitrary",), vmem_limit_bytes=VMEM_LIMIT),
        name="mixer",
    )(x2, ada3, row(g_pre), row(g_post), w_mix_in, row(lng), row(lnb), w_sp_pairs, b_sp_full,
      conv_w, row(conv_b), row(cng), row(cnb), row(goa), row(gob), w_mix_out)


def kernel(x, c, w_ada, b_ada, g_pre_f1, g_post_f1, w_f1_in, w_f1_out, g_pre_m, g_post_m, w_mix_in,
           gmlp_norm_g, gmlp_norm_b, w_spatial, b_spatial, conv_w, conv_b, conv_norm_g, conv_norm_b,
           g_out_a, g_out_b, w_mix_out, g_pre_f2, g_post_f2, w_f2_in, w_f2_out):
    bsz, seq, d = x.shape
    depth = w_ada.shape[0]
    w_a = gmlp_norm_g.shape[1]
    hd_a = w_a // H_A
    assert seq % FFN_TM == 0 and seq % MIX_TM == 0 and MIX_TM % CHUNK == 0
    assert 2 * hd_a == LANES and w_spatial.shape[2:] == (CHUNK, CHUNK)
    assert CONV_HALO >= CONV_K - 1 and MIX_TM % CONV_ROWS == 0

    x2 = x.reshape(bsz * seq, d)
    for l in range(depth):
        ada3 = _ada(c, w_ada[l], b_ada[l]).reshape(bsz, N_MOD, d)
        x2 = _ffn(x2, ada3, g_pre_f1[l], g_post_f1[l], w_f1_in[l].astype(_BF16),
                  w_f1_out[l].astype(_BF16), mod_row=0, seq=seq)
        w_sp_pairs = w_spatial[l].reshape(H_A // 2, 2, CHUNK, CHUNK).transpose(0, 2, 1, 3)
        w_sp_pairs = w_sp_pairs.reshape(H_A // 2, CHUNK, 2 * CHUNK)
        b_sp_full = jnp.repeat(b_spatial[l].T, hd_a, axis=1)
        x2 = _mixer(x2, ada3, g_pre_m[l], g_post_m[l], w_mix_in[l].astype(_BF16), gmlp_norm_g[l],
                    gmlp_norm_b[l], w_sp_pairs, b_sp_full, conv_w[l], conv_b[l], conv_norm_g[l],
                    conv_norm_b[l], g_out_a[l], g_out_b[l], w_mix_out[l].astype(_BF16), seq=seq)
        x2 = _ffn(x2, ada3, g_pre_f2[l], g_post_f2[l], w_f2_in[l].astype(_BF16),
                  w_f2_out[l].astype(_BF16), mod_row=6, seq=seq)
    return x2.reshape(bsz, seq, d)
```

```python
import functools

import jax
import jax.numpy as jnp
from jax import lax
from jax.experimental import pallas as pl
from jax.experimental.pallas import tpu as pltpu

N_MOD = 9
H_A = 8
CHUNK = 128
CONV_K = 31
EPS = 1e-6
HALF = 0.5

CONV_HALO = 32
CONV_ROWS = 64
LANES = 128
SUBLANES = 8

FFN_TM = 512
MIX_TM = 512
ADA_TN = 1024
VMEM_LIMIT = 56 * 1024 * 1024

_F32 = jnp.float32
_BF16 = jnp.bfloat16


def _rms(x, g):
    return x * lax.rsqrt(jnp.mean(x * x, axis=-1, keepdims=True) + EPS) * g


def _layer_norm(x, g, b):
    mu = jnp.mean(x, axis=-1, keepdims=True)
    xc = x - mu
    var = jnp.mean(xc * xc, axis=-1, keepdims=True)
    return xc * lax.rsqrt(var + EPS) * g + b


def _silu(x):
    return x * jax.nn.sigmoid(x)


def _dot(a, b):
    return jnp.dot(a, b, preferred_element_type=_F32)


def _ada_kernel(c_ref, w_ref, b_ref, o_ref):
    s = _silu(c_ref[...]).astype(_BF16)
    o_ref[...] = _dot(s, w_ref[...].astype(_BF16)) + b_ref[...]


def _ada(c, w, b):
    bsz, d = c.shape
    n = w.shape[1]
    return pl.pallas_call(
        _ada_kernel,
        grid=(n // ADA_TN,),
        in_specs=[
            pl.BlockSpec((bsz, d), lambda j: (0, 0)),
            pl.BlockSpec((d, ADA_TN), lambda j: (0, j)),
            pl.BlockSpec((1, ADA_TN), lambda j: (0, j)),
        ],
        out_specs=pl.BlockSpec((bsz, ADA_TN), lambda j: (0, j)),
        out_shape=jax.ShapeDtypeStruct((bsz, n), _F32),
        compiler_params=pltpu.CompilerParams(
            dimension_semantics=("parallel",), vmem_limit_bytes=VMEM_LIMIT),
        name="ada_proj",
    )(c, w, b.reshape(1, n))


def _ffn_kernel(x_ref, ada_ref, gpre_ref, gpost_ref, win_ref, wout_ref, o_ref, *, mod_row, d_ff):
    x = x_ref[...]
    ada = ada_ref[0]
    shift = ada[mod_row:mod_row + 1]
    scale = ada[mod_row + 1:mod_row + 2]
    gate = ada[mod_row + 2:mod_row + 3]
    h = (_rms(x, gpre_ref[...] * (1 + scale)) + shift).astype(_BF16)
    g = _dot(h, win_ref[:, :d_ff])
    u = _dot(h, win_ref[:, d_ff:])
    act = (_silu(g) * u).astype(_BF16)
    y = _dot(act, wout_ref[...])
    o_ref[...] = x + _rms(y, (HALF * gate) * gpost_ref[...])


def _resident(shape):
    return pl.BlockSpec(shape, lambda i: (0,) * len(shape), pipeline_mode=pl.Buffered(1))


def _ffn(x2, ada3, g_pre, g_post, w_in, w_out, *, mod_row, seq):
    m, d = x2.shape
    d_ff = w_out.shape[0]
    tm = FFN_TM
    tiles_per_seq = seq // tm
    return pl.pallas_call(
        functools.partial(_ffn_kernel, mod_row=mod_row, d_ff=d_ff),
        grid=(m // tm,),
        in_specs=[
            pl.BlockSpec((tm, d), lambda i: (i, 0)),
            pl.BlockSpec((1, N_MOD, d), lambda i: (i // tiles_per_seq, 0, 0)),
            _resident((1, d)),
            _resident((1, d)),
            _resident((d, 2 * d_ff)),
            _resident((d_ff, d)),
        ],
        out_specs=pl.BlockSpec((tm, d), lambda i: (i, 0)),
        out_shape=jax.ShapeDtypeStruct((m, d), _F32),
        compiler_params=pltpu.CompilerParams(
            dimension_semantics=("parallel",), vmem_limit_bytes=VMEM_LIMIT),
        name=f"ffn_mod{mod_row}",
    )(x2, ada3, g_pre.reshape(1, d), g_post.reshape(1, d), w_in, w_out)


def _mixer_kernel(x_ref, ada_ref, gpre_ref, gpost_ref, wmi_ref, lng_ref, lnb_ref, wsp_ref, bsp_ref,
                  cw_ref, cb_ref, cng_ref, cnb_ref, goa_ref, gob_ref, wmo_ref, o_ref, gbuf,
                  *, tiles_per_seq):
    tm, d = x_ref.shape
    w_a = lng_ref.shape[1]
    w_b = cw_ref.shape[1]
    n_pairs = w_a // LANES

    @pl.when(pl.program_id(0) % tiles_per_seq == 0)
    def _():
        gbuf[0:CONV_HALO, :] = jnp.zeros((CONV_HALO, w_b), _F32)

    x = x_ref[...]
    ada = ada_ref[0]
    shift, scale, gate = ada[3:4], ada[4:5], ada[5:6]
    h = (_rms(x, gpre_ref[...] * (1 + scale)) + shift).astype(_BF16)
    proj = _dot(h, wmi_ref[...])
    u = proj[:, :w_a]
    v = proj[:, w_a:2 * w_a]
    a = proj[:, 2 * w_a:2 * w_a + w_b]
    g = proj[:, 2 * w_a + w_b:]

    v = _layer_norm(v, lng_ref[...], lnb_ref[...])
    t_idx = lax.broadcasted_iota(jnp.int32, (CHUNK, 2 * CHUNK), 0)
    s_idx = lax.broadcasted_iota(jnp.int32, (CHUNK, 2 * CHUNK), 1) % CHUNK
    lane = lax.broadcasted_iota(jnp.int32, (CHUNK, LANES), 1)
    left = lane < (LANES // 2)
    w_pairs = [jnp.where(s_idx <= t_idx, wsp_ref[p], 0.0).astype(_BF16) for p in range(n_pairs)]
    z_rows = []
    for n in range(tm // CHUNK):
        z_cols = []
        for p in range(n_pairs):
            vp = v[n * CHUNK:(n + 1) * CHUNK, p * LANES:(p + 1) * LANES]
            rhs = jnp.concatenate([jnp.where(left, vp, 0.0), jnp.where(left, 0.0, vp)], axis=0)
            z_cols.append(_dot(w_pairs[p], rhs.astype(_BF16)))
        z_rows.append(jnp.concatenate(z_cols, axis=1) + bsp_ref[...])
    y_a = _rms(u * jnp.concatenate(z_rows, axis=0), goa_ref[...])

    gbuf[CONV_HALO:CONV_HALO + tm, :] = a * jax.nn.sigmoid(g)
    first_tap = CONV_HALO - (CONV_K - 1)
    conv_rows = []
    for r in range(tm // CONV_ROWS):
        r0 = r * CONV_ROWS
        conv_cols = []
        for c in range(w_b // LANES):
            cols = slice(c * LANES, (c + 1) * LANES)
            acc = None
            for rho in range(SUBLANES):
                part = None
                for k in range(CONV_K):
                    off = first_tap + k
                    if off % SUBLANES != rho:
                        continue
                    lo = r0 + off - rho
                    rows = CONV_ROWS + (SUBLANES if rho else 0)
                    term = gbuf[lo:lo + rows, cols] * cw_ref[k:k + 1, cols]
                    part = term if part is None else part + term
                part = part[rho:rho + CONV_ROWS]
                acc = part if acc is None else acc + part
            conv_cols.append(acc)
        conv_rows.append(jnp.concatenate(conv_cols, axis=1))
    conv = jnp.concatenate(conv_rows, axis=0) + cb_ref[...]
    gbuf[0:CONV_HALO, :] = gbuf[tm:tm + CONV_HALO, :]
    y_b = _rms(_silu(_layer_norm(conv, cng_ref[...], cnb_ref[...])), gob_ref[...])

    y = jnp.concatenate([y_a, y_b], axis=1).astype(_BF16)
    o_ref[...] = x + _rms(_dot(y, wmo_ref[...]), gate * gpost_ref[...])


def _mixer(x2, ada3, g_pre, g_post, w_mix_in, lng, lnb, w_sp_pairs, b_sp_full, conv_w, conv_b,
           cng, cnb, goa, gob, w_mix_out, *, seq):
    m, d = x2.shape
    w_a = lng.shape[0]
    w_b = conv_w.shape[1]
    tm = MIX_TM
    tiles_per_seq = seq // tm
    row = lambda v: v.reshape(1, -1)
    return pl.pallas_call(
        functools.partial(_mixer_kernel, tiles_per_seq=tiles_per_seq),
        grid=(m // tm,),
        in_specs=[
            pl.BlockSpec((tm, d), lambda i: (i, 0)),
            pl.BlockSpec((1, N_MOD, d), lambda i: (i // tiles_per_seq, 0, 0)),
            _resident((1, d)),
            _resident((1, d)),
            _resident(w_mix_in.shape),
            _resident((1, w_a)),
            _resident((1, w_a)),
            _resident(w_sp_pairs.shape),
            _resident(b_sp_full.shape),
            _resident(conv_w.shape),
            _resident((1, w_b)),
            _resident((1, w_b)),
            _resident((1, w_b)),
            _resident((1, w_a)),
            _resident((1, w_b)),
            _resident(w_mix_out.shape),
        ],
        out_specs=pl.BlockSpec((tm, d), lambda i: (i, 0)),
        out_shape=jax.ShapeDtypeStruct((m, d), _F32),
        scratch_shapes=[pltpu.VMEM((CONV_HALO + tm, w_b), _F32)],
        compiler_params=pltpu.CompilerParams(
            dimension_semantics=("arbitrary",), vmem_limit_bytes=VMEM_LIMIT),
        name="mixer",
    )(x2, ada3, row(g_pre), row(g_post), w_mix_in, row(lng), row(lnb), w_sp_pairs, b_sp_full,
      conv_w, row(conv_b), row(cng), row(cnb), row(goa), row(gob), w_mix_out)


def kernel(x, c, w_ada, b_ada, g_pre_f1, g_post_f1, w_f1_in, w_f1_out, g_pre_m, g_post_m, w_mix_in,
           gmlp_norm_g, gmlp_norm_b, w_spatial, b_spatial, conv_w, conv_b, conv_norm_g, conv_norm_b,
           g_out_a, g_out_b, w_mix_out, g_pre_f2, g_post_f2, w_f2_in, w_f2_out):
    bsz, seq, d = x.shape
    depth = w_ada.shape[0]
    w_a = gmlp_norm_g.shape[1]
    hd_a = w_a // H_A
    assert seq % FFN_TM == 0 and seq % MIX_TM == 0 and MIX_TM % CHUNK == 0
    assert 2 * hd_a == LANES and w_spatial.shape[2:] == (CHUNK, CHUNK)
    assert CONV_HALO >= CONV_K - 1 and MIX_TM % CONV_ROWS == 0

    x2 = x.reshape(bsz * seq, d)
    for l in range(depth):
        ada3 = _ada(c, w_ada[l], b_ada[l]).reshape(bsz, N_MOD, d)
        x2 = _ffn(x2, ada3, g_pre_f1[l], g_post_f1[l], w_f1_in[l].astype(_BF16),
                  w_f1_out[l].astype(_BF16), mod_row=0, seq=seq)
        w_sp_pairs = w_spatial[l].reshape(H_A // 2, 2, CHUNK, CHUNK).transpose(0, 2, 1, 3)
        w_sp_pairs = w_sp_pairs.reshape(H_A // 2, CHUNK, 2 * CHUNK)
        b_sp_full = jnp.repeat(b_spatial[l].T, hd_a, axis=1)
        x2 = _mixer(x2, ada3, g_pre_m[l], g_post_m[l], w_mix_in[l].astype(_BF16), gmlp_norm_g[l],
                    gmlp_norm_b[l], w_sp_pairs, b_sp_full, conv_w[l], conv_b[l], conv_norm_g[l],
                    conv_norm_b[l], g_out_a[l], g_out_b[l], w_mix_out[l].astype(_BF16), seq=seq)
        x2 = _ffn(x2, ada3, g_pre_f2[l], g_post_f2[l], w_f2_in[l].astype(_BF16),
                  w_f2_out[l].astype(_BF16), mod_row=6, seq=seq)
    return x2.reshape(bsz, seq, d)
```

```python
import functools

import jax
import jax.numpy as jnp
from jax import lax
from jax.experimental import pallas as pl
from jax.experimental.pallas import tpu as pltpu

N_MOD = 9
H_A = 8
CHUNK = 128
CONV_K = 31
EPS = 1e-6
HALF = 0.5

CONV_HALO = 32
CONV_ROWS = 64
LANES = 128
SUBLANES = 8

FFN_TM = 512
FFN_SUB = 256
MIX_TM = 512
MIX_SUB = 256
ADA_TN = 1024
VMEM_LIMIT = 56 * 1024 * 1024

_F32 = jnp.float32
_BF16 = jnp.bfloat16


def _rms(x, g):
    return x * lax.rsqrt(jnp.mean(x * x, axis=-1, keepdims=True) + EPS) * g


def _layer_norm(x, g, b):
    mu = jnp.mean(x, axis=-1, keepdims=True)
    xc = x - mu
    var = jnp.mean(xc * xc, axis=-1, keepdims=True)
    return xc * lax.rsqrt(var + EPS) * g + b


def _silu(x):
    return x * jax.nn.sigmoid(x)


def _dot(a, b):
    return jnp.dot(a, b, preferred_element_type=_F32)


def _ada_kernel(c_ref, w_ref, b_ref, o_ref):
    s = _silu(c_ref[...]).astype(_BF16)
    o_ref[...] = _dot(s, w_ref[...].astype(_BF16)) + b_ref[...]


def _ada(c, w, b):
    bsz, d = c.shape
    n = w.shape[1]
    return pl.pallas_call(
        _ada_kernel,
        grid=(n // ADA_TN,),
        in_specs=[
            pl.BlockSpec((bsz, d), lambda j: (0, 0)),
            pl.BlockSpec((d, ADA_TN), lambda j: (0, j)),
            pl.BlockSpec((1, ADA_TN), lambda j: (0, j)),
        ],
        out_specs=pl.BlockSpec((bsz, ADA_TN), lambda j: (0, j)),
        out_shape=jax.ShapeDtypeStruct((bsz, n), _F32),
        compiler_params=pltpu.CompilerParams(
            dimension_semantics=("parallel",), vmem_limit_bytes=VMEM_LIMIT),
        name="ada_proj",
    )(c, w, b.reshape(1, n))


def _ffn_kernel(x_ref, ada_ref, gpre_ref, gpost_ref, win_ref, wout_ref, o_ref, *, mod_row, d_ff):
    ada = ada_ref[0]
    shift = ada[mod_row:mod_row + 1]
    scale = ada[mod_row + 1:mod_row + 2]
    gate = ada[mod_row + 2:mod_row + 3]
    g_in = gpre_ref[...] * (1 + scale)
    g_out = (HALF * gate) * gpost_ref[...]
    for s in range(x_ref.shape[0] // FFN_SUB):
        rows = slice(s * FFN_SUB, (s + 1) * FFN_SUB)
        x = x_ref[rows, :]
        h = (_rms(x, g_in) + shift).astype(_BF16)
        g = _dot(h, win_ref[:, :d_ff])
        u = _dot(h, win_ref[:, d_ff:])
        act = (_silu(g) * u).astype(_BF16)
        y = _dot(act, wout_ref[...])
        o_ref[rows, :] = x + _rms(y, g_out)


def _resident(shape):
    return pl.BlockSpec(shape, lambda i: (0,) * len(shape), pipeline_mode=pl.Buffered(1))


def _ffn(x2, ada3, g_pre, g_post, w_in, w_out, *, mod_row, seq):
    m, d = x2.shape
    d_ff = w_out.shape[0]
    tm = FFN_TM
    tiles_per_seq = seq // tm
    return pl.pallas_call(
        functools.partial(_ffn_kernel, mod_row=mod_row, d_ff=d_ff),
        grid=(m // tm,),
        in_specs=[
            pl.BlockSpec((tm, d), lambda i: (i, 0)),
            pl.BlockSpec((1, N_MOD, d), lambda i: (i // tiles_per_seq, 0, 0)),
            _resident((1, d)),
            _resident((1, d)),
            _resident((d, 2 * d_ff)),
            _resident((d_ff, d)),
        ],
        out_specs=pl.BlockSpec((tm, d), lambda i: (i, 0)),
        out_shape=jax.ShapeDtypeStruct((m, d), _F32),
        compiler_params=pltpu.CompilerParams(
            dimension_semantics=("parallel",), vmem_limit_bytes=VMEM_LIMIT),
        name=f"ffn_mod{mod_row}",
    )(x2, ada3, g_pre.reshape(1, d), g_post.reshape(1, d), w_in, w_out)


def _mixer_kernel(x_ref, ada_ref, gpre_ref, gpost_ref, wmi_ref, lng_ref, lnb_ref, wsp_ref, bsp_ref,
                  cw_ref, cb_ref, cng_ref, cnb_ref, goa_ref, gob_ref, wmo_ref, o_ref, gbuf,
                  *, tiles_per_seq):
    tm, d = x_ref.shape
    w_a = lng_ref.shape[1]
    w_b = cw_ref.shape[1]
    n_pairs = w_a // LANES

    @pl.when(pl.program_id(0) % tiles_per_seq == 0)
    def _():
        gbuf[0:CONV_HALO, :] = jnp.zeros((CONV_HALO, w_b), _F32)

    ada = ada_ref[0]
    shift, scale, gate = ada[3:4], ada[4:5], ada[5:6]
    g_in = gpre_ref[...] * (1 + scale)
    g_out = gate * gpost_ref[...]

    t_idx = lax.broadcasted_iota(jnp.int32, (CHUNK, 2 * CHUNK), 0)
    s_idx = lax.broadcasted_iota(jnp.int32, (CHUNK, 2 * CHUNK), 1) % CHUNK
    left = lax.broadcasted_iota(jnp.int32, (CHUNK, LANES), 1) < (LANES // 2)
    w_pairs = [jnp.where(s_idx <= t_idx, wsp_ref[p], 0.0).astype(_BF16) for p in range(n_pairs)]
    first_tap = CONV_HALO - (CONV_K - 1)

    def project(t0):
        x = x_ref[t0:t0 + MIX_SUB, :]
        h = (_rms(x, g_in) + shift).astype(_BF16)
        return x, _dot(h, wmi_ref[...])

    n_sub = tm // MIX_SUB
    nxt = project(0)
    for sub in range(n_sub):
        t0 = sub * MIX_SUB
        x, proj = nxt
        if sub + 1 < n_sub:
            nxt = project(t0 + MIX_SUB)
        u = proj[:, :w_a]
        v = proj[:, w_a:2 * w_a]
        a = proj[:, 2 * w_a:2 * w_a + w_b]
        g = proj[:, 2 * w_a + w_b:]

        v = _layer_norm(v, lng_ref[...], lnb_ref[...])
        z_rows = []
        for n in range(MIX_SUB // CHUNK):
            z_cols = []
            for p in range(n_pairs):
                vp = v[n * CHUNK:(n + 1) * CHUNK, p * LANES:(p + 1) * LANES]
                rhs = jnp.concatenate([jnp.where(left, vp, 0.0), jnp.where(left, 0.0, vp)], axis=0)
                z_cols.append(_dot(w_pairs[p], rhs.astype(_BF16)))
            z_rows.append(jnp.concatenate(z_cols, axis=1) + bsp_ref[...])
        y_a = _rms(u * jnp.concatenate(z_rows, axis=0), goa_ref[...])

        gbuf[CONV_HALO + t0:CONV_HALO + t0 + MIX_SUB, :] = a * jax.nn.sigmoid(g)
        conv_rows = []
        for r in range(MIX_SUB // CONV_ROWS):
            r0 = t0 + r * CONV_ROWS
            conv_cols = []
            for c in range(w_b // LANES):
                cols = slice(c * LANES, (c + 1) * LANES)
                acc = None
                for rho in range(SUBLANES):
                    part = None
                    for k in range(CONV_K):
                        off = first_tap + k
                        if off % SUBLANES != rho:
                            continue
                        lo = r0 + off - rho
                        rows = CONV_ROWS + (SUBLANES if rho else 0)
                        term = gbuf[lo:lo + rows, cols] * cw_ref[k:k + 1, cols]
                        part = term if part is None else part + term
                    part = part[rho:rho + CONV_ROWS]
                    acc = part if acc is None else acc + part
                conv_cols.append(acc)
            conv_rows.append(jnp.concatenate(conv_cols, axis=1))
        conv = jnp.concatenate(conv_rows, axis=0) + cb_ref[...]
        y_b = _rms(_silu(_layer_norm(conv, cng_ref[...], cnb_ref[...])), gob_ref[...])

        y = jnp.concatenate([y_a, y_b], axis=1).astype(_BF16)
        o_ref[t0:t0 + MIX_SUB, :] = x + _rms(_dot(y, wmo_ref[...]), g_out)

    gbuf[0:CONV_HALO, :] = gbuf[tm:tm + CONV_HALO, :]


def _mixer(x2, ada3, g_pre, g_post, w_mix_in, lng, lnb, w_sp_pairs, b_sp_full, conv_w, conv_b,
           cng, cnb, goa, gob, w_mix_out, *, seq):
    m, d = x2.shape
    w_a = lng.shape[0]
    w_b = conv_w.shape[1]
    tm = MIX_TM
    tiles_per_seq = seq // tm
    row = lambda v: v.reshape(1, -1)
    return pl.pallas_call(
        functools.partial(_mixer_kernel, tiles_per_seq=tiles_per_seq),
        grid=(m // tm,),
        in_specs=[
            pl.BlockSpec((tm, d), lambda i: (i, 0)),
            pl.BlockSpec((1, N_MOD, d), lambda i: (i // tiles_per_seq, 0, 0)),
            _resident((1, d)),
            _resident((1, d)),
            _resident(w_mix_in.shape),
            _resident((1, w_a)),
            _resident((1, w_a)),
            _resident(w_sp_pairs.shape),
            _resident(b_sp_full.shape),
            _resident(conv_w.shape),
            _resident((1, w_b)),
            _resident((1, w_b)),
            _resident((1, w_b)),
            _resident((1, w_a)),
            _resident((1, w_b)),
            _resident(w_mix_out.shape),
        ],
        out_specs=pl.BlockSpec((tm, d), lambda i: (i, 0)),
        out_shape=jax.ShapeDtypeStruct((m, d), _F32),
        scratch_shapes=[pltpu.VMEM((CONV_HALO + tm, w_b), _F32)],
        compiler_params=pltpu.CompilerParams(
            dimension_semantics=("arbitrary",), vmem_limit_bytes=VMEM_LIMIT),
        name="mixer",
    )(x2, ada3, row(g_pre), row(g_post), w_mix_in, row(lng), row(lnb), w_sp_pairs, b_sp_full,
      conv_w, row(conv_b), row(cng), row(cnb), row(goa), row(gob), w_mix_out)


def kernel(x, c, w_ada, b_ada, g_pre_f1, g_post_f1, w_f1_in, w_f1_out, g_pre_m, g_post_m, w_mix_in,
           gmlp_norm_g, gmlp_norm_b, w_spatial, b_spatial, conv_w, conv_b, conv_norm_g, conv_norm_b,
           g_out_a, g_out_b, w_mix_out, g_pre_f2, g_post_f2, w_f2_in, w_f2_out):
    bsz, seq, d = x.shape
    depth = w_ada.shape[0]
    w_a = gmlp_norm_g.shape[1]
    hd_a = w_a // H_A
    assert seq % FFN_TM == 0 and FFN_TM % FFN_SUB == 0
    assert seq % MIX_TM == 0 and MIX_TM % MIX_SUB == 0 and MIX_SUB % CHUNK == 0
    assert 2 * hd_a == LANES and w_spatial.shape[2:] == (CHUNK, CHUNK)
    assert CONV_HALO >= CONV_K - 1 and MIX_SUB % CONV_ROWS == 0

    x2 = x.reshape(bsz * seq, d)
    for l in range(depth):
        ada3 = _ada(c, w_ada[l], b_ada[l]).reshape(bsz, N_MOD, d)
        x2 = _ffn(x2, ada3, g_pre_f1[l], g_post_f1[l], w_f1_in[l].astype(_BF16),
                  w_f1_out[l].astype(_BF16), mod_row=0, seq=seq)
        w_sp_pairs = w_spatial[l].reshape(H_A // 2, 2, CHUNK, CHUNK).transpose(0, 2, 1, 3)
        w_sp_pairs = w_sp_pairs.reshape(H_A // 2, CHUNK, 2 * CHUNK)
        b_sp_full = jnp.repeat(b_spatial[l].T, hd_a, axis=1)
        x2 = _mixer(x2, ada3, g_pre_m[l], g_post_m[l], w_mix_in[l].astype(_BF16), gmlp_norm_g[l],
                    gmlp_norm_b[l], w_sp_pairs, b_sp_full, conv_w[l], conv_b[l], conv_norm_g[l],
                    conv_norm_b[l], g_out_a[l], g_out_b[l], w_mix_out[l].astype(_BF16), seq=seq)
        x2 = _ffn(x2, ada3, g_pre_f2[l], g_post_f2[l], w_f2_in[l].astype(_BF16),
                  w_f2_out[l].astype(_BF16), mod_row=6, seq=seq)
    return x2.reshape(bsz, seq, d)
```

```python
import functools

import jax
import jax.numpy as jnp
from jax import lax
from jax.experimental import pallas as pl
from jax.experimental.pallas import tpu as pltpu

N_MOD = 9
H_A = 8
CHUNK = 128
CONV_K = 31
EPS = 1e-6
HALF = 0.5

CONV_HALO = 32
CONV_STRIDE = 34
CONV_GROUP = 17
LOOP_UNITS = 4
LANES = 128
SUBLANES = 8
MXU_N = 256

FFN_TM = 1024
FFN_SUB = 256
MIX_TM = 512
MIX_SUB = 256
ADA_TN = 1024
VMEM_LIMIT = 56 * 1024 * 1024

_F32 = jnp.float32
_BF16 = jnp.bfloat16


def _rms(x, g):
    return x * lax.rsqrt(jnp.mean(x * x, axis=-1, keepdims=True) + EPS) * g


def _layer_norm(x, g, b):
    mu = jnp.mean(x, axis=-1, keepdims=True)
    xc = x - mu
    var = jnp.mean(xc * xc, axis=-1, keepdims=True)
    return xc * lax.rsqrt(var + EPS) * g + b


def _silu(x):
    return x * jax.nn.sigmoid(x)


def _dot(a, b):
    return jnp.dot(a, b, preferred_element_type=_F32)


def _ada_kernel(c_ref, w_ref, b_ref, o_ref):
    s = _silu(c_ref[...]).astype(_BF16)
    o_ref[...] = _dot(s, w_ref[...].astype(_BF16)) + b_ref[...]


def _ada(c, w, b):
    bsz, d = c.shape
    n = w.shape[1]
    return pl.pallas_call(
        _ada_kernel,
        grid=(n // ADA_TN,),
        in_specs=[
            pl.BlockSpec((bsz, d), lambda j: (0, 0)),
            pl.BlockSpec((d, ADA_TN), lambda j: (0, j)),
            pl.BlockSpec((1, ADA_TN), lambda j: (0, j)),
        ],
        out_specs=pl.BlockSpec((bsz, ADA_TN), lambda j: (0, j)),
        out_shape=jax.ShapeDtypeStruct((bsz, n), _F32),
        compiler_params=pltpu.CompilerParams(
            dimension_semantics=("parallel",), vmem_limit_bytes=VMEM_LIMIT),
        name="ada_proj",
    )(c, w, b.reshape(1, n))


def _ffn_kernel(x_ref, ada_ref, gpre_ref, gpost_ref, win_ref, wout_ref, o_ref, *, mod_row, d_ff):
    ada = ada_ref[0]
    shift = ada[mod_row:mod_row + 1]
    scale = ada[mod_row + 1:mod_row + 2]
    gate = ada[mod_row + 2:mod_row + 3]
    g_in = gpre_ref[...] * (1 + scale)
    g_out = (HALF * gate) * gpost_ref[...]
    for s in range(x_ref.shape[0] // FFN_SUB):
        rows = slice(s * FFN_SUB, (s + 1) * FFN_SUB)
        x = x_ref[rows, :]
        h = (_rms(x, g_in) + shift).astype(_BF16)
        g = _dot(h, win_ref[:, :d_ff])
        u = _dot(h, win_ref[:, d_ff:])
        act = (_silu(g) * u).astype(_BF16)
        y = _dot(act, wout_ref[...])
        o_ref[rows, :] = x + _rms(y, g_out)


def _resident(shape):
    return pl.BlockSpec(shape, lambda i: (0,) * len(shape), pipeline_mode=pl.Buffered(1))


def _ffn(x2, ada3, g_pre, g_post, w_in, w_out, *, mod_row, seq):
    m, d = x2.shape
    d_ff = w_out.shape[0]
    tm = FFN_TM
    tiles_per_seq = seq // tm
    return pl.pallas_call(
        functools.partial(_ffn_kernel, mod_row=mod_row, d_ff=d_ff),
        grid=(m // tm,),
        in_specs=[
            pl.BlockSpec((tm, d), lambda i: (i, 0)),
            pl.BlockSpec((1, N_MOD, d), lambda i: (i // tiles_per_seq, 0, 0)),
            _resident((1, d)),
            _resident((1, d)),
            _resident((d, 2 * d_ff)),
            _resident((d_ff, d)),
        ],
        out_specs=pl.BlockSpec((tm, d), lambda i: (i, 0)),
        out_shape=jax.ShapeDtypeStruct((m, d), _F32),
        compiler_params=pltpu.CompilerParams(
            dimension_semantics=("parallel",), vmem_limit_bytes=VMEM_LIMIT),
        name=f"ffn_mod{mod_row}",
    )(x2, ada3, g_pre.reshape(1, d), g_post.reshape(1, d), w_in, w_out)


def _mixer_kernel(x_ref, xn_ref, ada_ref, adan_ref, gpre_ref, gpost_ref, wmi_ref, lng_ref, lnb_ref, wsp_ref,
                  bsp_ref, cw_ref, cb_ref, cng_ref, cnb_ref, goa_ref, gob_ref, wmo_ref, o_ref,
                  gbuf, cbuf, ybuf, hbuf, pbuf, ycat, *, tiles_per_seq):
    tm, d = x_ref.shape
    w_a = lng_ref.shape[1]
    n_pairs = w_a // LANES
    n_lane_tiles, _, _ = cw_ref.shape
    w_b = n_lane_tiles * LANES
    n_pieces = wmi_ref.shape[0]
    n_sub = tm // MIX_SUB
    n_groups = CONV_STRIDE // CONV_GROUP
    assert n_pieces == n_lane_tiles * n_groups and n_pieces * MXU_N == 2 * w_a + 2 * w_b
    step = pl.program_id(0)

    def modulated(x, ada):
        shift, scale = ada[3:4], ada[4:5]
        return (_rms(x, gpre_ref[...] * (1 + scale)) + shift).astype(_BF16)

    @pl.when(step == 0)
    def _():
        gbuf[...] = jnp.zeros(gbuf.shape, _F32)
        h0 = modulated(x_ref[0:MIX_SUB, :], ada_ref[0])
        for j in range(n_pieces):
            pbuf[0, j] = _dot(h0, wmi_ref[j])

    @pl.when(step % tiles_per_seq == 0)
    def _():
        gbuf[:, 0:CONV_HALO, :] = jnp.zeros((n_lane_tiles, CONV_HALO, LANES), _F32)

    g_out = ada_ref[0][5:6] * gpost_ref[...]
    t_idx = lax.broadcasted_iota(jnp.int32, (CHUNK, 2 * CHUNK), 0)
    s_idx = lax.broadcasted_iota(jnp.int32, (CHUNK, 2 * CHUNK), 1) % CHUNK
    left = lax.broadcasted_iota(jnp.int32, (CHUNK, LANES), 1) < (LANES // 2)
    w_pairs = [jnp.where(s_idx <= t_idx, wsp_ref[p], 0.0).astype(_BF16) for p in range(n_pairs)]

    for sub in range(n_sub):
        t0 = sub * MIX_SUB
        slot, nslot = sub % 2, (sub + 1) % 2
        proj = jnp.concatenate([pbuf[slot, j] for j in range(n_pieces)], axis=1)
        u = proj[:, :w_a]
        v = proj[:, w_a:2 * w_a]
        a = proj[:, 2 * w_a:2 * w_a + w_b]
        g = proj[:, 2 * w_a + w_b:]

        glu = a * jax.nn.sigmoid(g)
        for c in range(n_lane_tiles):
            gbuf[c, CONV_HALO + t0:CONV_HALO + t0 + MIX_SUB, :] = glu[:, c * LANES:(c + 1) * LANES]

        v = _layer_norm(v, lng_ref[...], lnb_ref[...])
        z_rows = []
        for n in range(MIX_SUB // CHUNK):
            z_cols = []
            for p in range(n_pairs):
                vp = v[n * CHUNK:(n + 1) * CHUNK, p * LANES:(p + 1) * LANES]
                rhs = jnp.concatenate([jnp.where(left, vp, 0.0), jnp.where(left, 0.0, vp)], axis=0)
                z_cols.append(_dot(w_pairs[p], rhs.astype(_BF16)))
            z_rows.append(jnp.concatenate(z_cols, axis=1) + bsp_ref[...])
        ycat[:, 0:w_a] = _rms(u * jnp.concatenate(z_rows, axis=0), goa_ref[...]).astype(_BF16)

        if sub + 1 < n_sub:
            hbuf[...] = modulated(x_ref[t0 + MIX_SUB:t0 + 2 * MIX_SUB, :], ada_ref[0])
        else:
            hbuf[...] = modulated(xn_ref[...], adan_ref[0])

        def unit(it):
            pbuf[nslot, it] = _dot(hbuf[...], wmi_ref[it])
            c = lax.shift_right_logical(it, n_groups.bit_length() - 1)
            r0 = (it & (n_groups - 1)) * CONV_GROUP
            accs = [None] * CONV_GROUP
            for s in range(CONV_GROUP + CONV_K - 1):
                start = CONV_HALO + t0 - (CONV_K - 1) + r0 + s
                rows = gbuf[c, pl.ds(start, SUBLANES, stride=CONV_STRIDE), :]
                for i in range(CONV_GROUP):
                    k = s - i
                    if 0 <= k < CONV_K:
                        term = rows * cw_ref[c, k:k + 1, :]
                        accs[i] = term if accs[i] is None else accs[i] + term
            for i in range(CONV_GROUP):
                cbuf[c, pl.ds(pl.multiple_of((r0 + i) * SUBLANES, SUBLANES), SUBLANES), :] = accs[i]

        def units(j, carry):
            for q in range(LOOP_UNITS):
                unit(j * LOOP_UNITS + q)
            return carry
        lax.fori_loop(0, n_pieces // LOOP_UNITS, units, 0)

        conv = jnp.concatenate([cbuf[c] for c in range(n_lane_tiles)], axis=1) + cb_ref[...]
        y_b = _rms(_silu(_layer_norm(conv, cng_ref[...], cnb_ref[...])), gob_ref[...])
        for r in range(CONV_STRIDE):
            for c in range(n_lane_tiles):
                ybuf[c, pl.ds(r, SUBLANES, stride=CONV_STRIDE), :] = (
                    y_b[r * SUBLANES:(r + 1) * SUBLANES, c * LANES:(c + 1) * LANES])
        for c in range(n_lane_tiles):
            ycat[:, w_a + c * LANES:w_a + (c + 1) * LANES] = ybuf[c, 0:MIX_SUB, :].astype(_BF16)

        o_ref[t0:t0 + MIX_SUB, :] = x_ref[t0:t0 + MIX_SUB, :] + _rms(_dot(ycat[...], wmo_ref[...]), g_out)

    gbuf[:, 0:CONV_HALO, :] = gbuf[:, tm:tm + CONV_HALO, :]


def _mixer(x2, ada3, g_pre, g_post, w_mix_in, lng, lnb, w_sp_pairs, b_sp_full, conv_w, conv_b,
           cng, cnb, goa, gob, w_mix_out, *, seq):
    m, d = x2.shape
    w_a = lng.shape[0]
    w_b = conv_w.shape[1]
    tm = MIX_TM
    n = m // tm
    n_sub = tm // MIX_SUB
    tiles_per_seq = seq // tm
    row = lambda v: v.reshape(1, -1)
    next_tile = lambda i: jnp.minimum(i + 1, n - 1)
    wmi_pieces = w_mix_in.reshape(d, -1, MXU_N).transpose(1, 0, 2)
    cw_slabs = conv_w.reshape(CONV_K, -1, LANES).transpose(1, 0, 2)
    n_lane_tiles = w_b // LANES
    return pl.pallas_call(
        functools.partial(_mixer_kernel, tiles_per_seq=tiles_per_seq),
        grid=(n,),
        in_specs=[
            pl.BlockSpec((tm, d), lambda i: (i, 0)),
            pl.BlockSpec((MIX_SUB, d), lambda i: (next_tile(i) * n_sub, 0)),
            pl.BlockSpec((1, N_MOD, d), lambda i: (i // tiles_per_seq, 0, 0)),
            pl.BlockSpec((1, N_MOD, d), lambda i: (next_tile(i) // tiles_per_seq, 0, 0)),
            _resident((1, d)),
            _resident((1, d)),
            _resident(wmi_pieces.shape),
            _resident((1, w_a)),
            _resident((1, w_a)),
            _resident(w_sp_pairs.shape),
            _resident(b_sp_full.shape),
            _resident(cw_slabs.shape),
            _resident((1, w_b)),
            _resident((1, w_b)),
            _resident((1, w_b)),
            _resident((1, w_a)),
            _resident((1, w_b)),
            _resident(w_mix_out.shape),
        ],
        out_specs=pl.BlockSpec((tm, d), lambda i: (i, 0)),
        out_shape=jax.ShapeDtypeStruct((m, d), _F32),
        scratch_shapes=[
            pltpu.VMEM((n_lane_tiles, CONV_HALO + tm + SUBLANES * CONV_STRIDE - MIX_SUB, LANES), _F32),
            pltpu.VMEM((n_lane_tiles, SUBLANES * CONV_STRIDE, LANES), _F32),
            pltpu.VMEM((n_lane_tiles, SUBLANES * CONV_STRIDE, LANES), _F32),
            pltpu.VMEM((MIX_SUB, d), _BF16),
            pltpu.VMEM((2, wmi_pieces.shape[0], MIX_SUB, MXU_N), _F32),
            pltpu.VMEM((MIX_SUB, w_a + w_b), _BF16),
        ],
        compiler_params=pltpu.CompilerParams(
            dimension_semantics=("arbitrary",), vmem_limit_bytes=VMEM_LIMIT),
        name="mixer",
    )(x2, x2, ada3, ada3, row(g_pre), row(g_post), wmi_pieces, row(lng), row(lnb), w_sp_pairs, b_sp_full,
      cw_slabs, row(conv_b), row(cng), row(cnb), row(goa), row(gob), w_mix_out)


def kernel(x, c, w_ada, b_ada, g_pre_f1, g_post_f1, w_f1_in, w_f1_out, g_pre_m, g_post_m, w_mix_in,
           gmlp_norm_g, gmlp_norm_b, w_spatial, b_spatial, conv_w, conv_b, conv_norm_g, conv_norm_b,
           g_out_a, g_out_b, w_mix_out, g_pre_f2, g_post_f2, w_f2_in, w_f2_out):
    bsz, seq, d = x.shape
    depth = w_ada.shape[0]
    w_a = gmlp_norm_g.shape[1]
    hd_a = w_a // H_A
    assert seq % FFN_TM == 0 and FFN_TM % FFN_SUB == 0
    assert seq % MIX_TM == 0 and MIX_TM % (2 * MIX_SUB) == 0 and MIX_SUB % CHUNK == 0
    assert 2 * hd_a == LANES and w_spatial.shape[2:] == (CHUNK, CHUNK)
    assert CONV_HALO >= CONV_K - 1 and SUBLANES * CONV_STRIDE >= MIX_SUB and CONV_STRIDE % CONV_GROUP == 0

    x2 = x.reshape(bsz * seq, d)
    for l in range(depth):
        ada3 = _ada(c, w_ada[l], b_ada[l]).reshape(bsz, N_MOD, d)
        x2 = _ffn(x2, ada3, g_pre_f1[l], g_post_f1[l], w_f1_in[l].astype(_BF16),
                  w_f1_out[l].astype(_BF16), mod_row=0, seq=seq)
        w_sp_pairs = w_spatial[l].reshape(H_A // 2, 2, CHUNK, CHUNK).transpose(0, 2, 1, 3)
        w_sp_pairs = w_sp_pairs.reshape(H_A // 2, CHUNK, 2 * CHUNK)
        b_sp_full = jnp.repeat(b_spatial[l].T, hd_a, axis=1)
        x2 = _mixer(x2, ada3, g_pre_m[l], g_post_m[l], w_mix_in[l].astype(_BF16), gmlp_norm_g[l],
                    gmlp_norm_b[l], w_sp_pairs, b_sp_full, conv_w[l], conv_b[l], conv_norm_g[l],
                    conv_norm_b[l], g_out_a[l], g_out_b[l], w_mix_out[l].astype(_BF16), seq=seq)
        x2 = _ffn(x2, ada3, g_pre_f2[l], g_post_f2[l], w_f2_in[l].astype(_BF16),
                  w_f2_out[l].astype(_BF16), mod_row=6, seq=seq)
    return x2.reshape(bsz, seq, d)
```

```python
import functools

import jax
import jax.numpy as jnp
from jax import lax
from jax.experimental import pallas as pl
from jax.experimental.pallas import tpu as pltpu

N_MOD = 9
H_A = 8
CHUNK = 128
CONV_K = 31
EPS = 1e-6
HALF = 0.5

CONV_HALO = 32
CONV_ROWS = 64
LANES = 128
SUBLANES = 8

FFN_TM = 1024
FFN_SUB = 256
MIX_TM = 1024
MIX_SUB = 256
ADA_TN = 1024
VMEM_LIMIT = 56 * 1024 * 1024

_F32 = jnp.float32
_BF16 = jnp.bfloat16


def _rms(x, g):
    return x * lax.rsqrt(jnp.mean(x * x, axis=-1, keepdims=True) + EPS) * g


def _layer_norm(x, g, b):
    mu = jnp.mean(x, axis=-1, keepdims=True)
    xc = x - mu
    var = jnp.mean(xc * xc, axis=-1, keepdims=True)
    return xc * lax.rsqrt(var + EPS) * g + b


def _silu(x):
    return x * jax.nn.sigmoid(x)


def _dot(a, b):
    return jnp.dot(a, b, preferred_element_type=_F32)


def _ada_kernel(c_ref, w_ref, b_ref, o_ref):
    s = _silu(c_ref[...]).astype(_BF16)
    o_ref[...] = _dot(s, w_ref[...].astype(_BF16)) + b_ref[...]


def _ada(c, w, b):
    bsz, d = c.shape
    n = w.shape[1]
    return pl.pallas_call(
        _ada_kernel,
        grid=(n // ADA_TN,),
        in_specs=[
            pl.BlockSpec((bsz, d), lambda j: (0, 0)),
            pl.BlockSpec((d, ADA_TN), lambda j: (0, j)),
            pl.BlockSpec((1, ADA_TN), lambda j: (0, j)),
        ],
        out_specs=pl.BlockSpec((bsz, ADA_TN), lambda j: (0, j)),
        out_shape=jax.ShapeDtypeStruct((bsz, n), _F32),
        compiler_params=pltpu.CompilerParams(
            dimension_semantics=("parallel",), vmem_limit_bytes=VMEM_LIMIT),
        name="ada_proj",
    )(c, w, b.reshape(1, n))


def _ffn_kernel(x_ref, ada_ref, gpre_ref, gpost_ref, win_ref, wout_ref, o_ref, *, mod_row, d_ff):
    ada = ada_ref[0]
    shift = ada[mod_row:mod_row + 1]
    scale = ada[mod_row + 1:mod_row + 2]
    gate = ada[mod_row + 2:mod_row + 3]
    g_in = gpre_ref[...] * (1 + scale)
    g_out = (HALF * gate) * gpost_ref[...]
    for s in range(x_ref.shape[0] // FFN_SUB):
        rows = slice(s * FFN_SUB, (s + 1) * FFN_SUB)
        x = x_ref[rows, :]
        h = (_rms(x, g_in) + shift).astype(_BF16)
        g = _dot(h, win_ref[:, :d_ff])
        u = _dot(h, win_ref[:, d_ff:])
        act = (_silu(g) * u).astype(_BF16)
        y = _dot(act, wout_ref[...])
        o_ref[rows, :] = x + _rms(y, g_out)


def _resident(shape):
    return pl.BlockSpec(shape, lambda i: (0,) * len(shape), pipeline_mode=pl.Buffered(1))


def _ffn(x2, ada3, g_pre, g_post, w_in, w_out, *, mod_row, seq):
    m, d = x2.shape
    d_ff = w_out.shape[0]
    tm = FFN_TM
    tiles_per_seq = seq // tm
    return pl.pallas_call(
        functools.partial(_ffn_kernel, mod_row=mod_row, d_ff=d_ff),
        grid=(m // tm,),
        in_specs=[
            pl.BlockSpec((tm, d), lambda i: (i, 0)),
            pl.BlockSpec((1, N_MOD, d), lambda i: (i // tiles_per_seq, 0, 0)),
            _resident((1, d)),
            _resident((1, d)),
            _resident((d, 2 * d_ff)),
            _resident((d_ff, d)),
        ],
        out_specs=pl.BlockSpec((tm, d), lambda i: (i, 0)),
        out_shape=jax.ShapeDtypeStruct((m, d), _F32),
        compiler_params=pltpu.CompilerParams(
            dimension_semantics=("parallel",), vmem_limit_bytes=VMEM_LIMIT),
        name=f"ffn_mod{mod_row}",
    )(x2, ada3, g_pre.reshape(1, d), g_post.reshape(1, d), w_in, w_out)


def _mixer_kernel(x_ref, ada_ref, gpre_ref, gpost_ref, wmi_ref, lng_ref, lnb_ref, wsp_ref, bsp_ref,
                  cw_ref, cb_ref, cng_ref, cnb_ref, goa_ref, gob_ref, wmo_ref, o_ref, gbuf,
                  *, tiles_per_seq):
    tm, d = x_ref.shape
    w_a = lng_ref.shape[1]
    w_b = cw_ref.shape[1]
    n_pairs = w_a // LANES

    @pl.when(pl.program_id(0) % tiles_per_seq == 0)
    def _():
        gbuf[0:CONV_HALO, :] = jnp.zeros((CONV_HALO, w_b), _F32)

    ada = ada_ref[0]
    shift, scale, gate = ada[3:4], ada[4:5], ada[5:6]
    g_in = gpre_ref[...] * (1 + scale)
    g_out = gate * gpost_ref[...]

    t_idx = lax.broadcasted_iota(jnp.int32, (CHUNK, 2 * CHUNK), 0)
    s_idx = lax.broadcasted_iota(jnp.int32, (CHUNK, 2 * CHUNK), 1) % CHUNK
    left = lax.broadcasted_iota(jnp.int32, (CHUNK, LANES), 1) < (LANES // 2)
    w_pairs = [jnp.where(s_idx <= t_idx, wsp_ref[p], 0.0).astype(_BF16) for p in range(n_pairs)]
    first_tap = CONV_HALO - (CONV_K - 1)

    def project(t0):
        x = x_ref[t0:t0 + MIX_SUB, :]
        h = (_rms(x, g_in) + shift).astype(_BF16)
        return x, _dot(h, wmi_ref[...])

    n_sub = tm // MIX_SUB
    nxt = project(0)
    for sub in range(n_sub):
        t0 = sub * MIX_SUB
        x, proj = nxt
        if sub + 1 < n_sub:
            nxt = project(t0 + MIX_SUB)
        u = proj[:, :w_a]
        v = proj[:, w_a:2 * w_a]
        a = proj[:, 2 * w_a:2 * w_a + w_b]
        g = proj[:, 2 * w_a + w_b:]

        v = _layer_norm(v, lng_ref[...], lnb_ref[...])
        z_rows = []
        for n in range(MIX_SUB // CHUNK):
            z_cols = []
            for p in range(n_pairs):
                vp = v[n * CHUNK:(n + 1) * CHUNK, p * LANES:(p + 1) * LANES]
                rhs = jnp.concatenate([jnp.where(left, vp, 0.0), jnp.where(left, 0.0, vp)], axis=0)
                z_cols.append(_dot(w_pairs[p], rhs.astype(_BF16)))
            z_rows.append(jnp.concatenate(z_cols, axis=1) + bsp_ref[...])
        y_a = _rms(u * jnp.concatenate(z_rows, axis=0), goa_ref[...])

        gbuf[CONV_HALO + t0:CONV_HALO + t0 + MIX_SUB, :] = a * jax.nn.sigmoid(g)
        conv_rows = []
        for r in range(MIX_SUB // CONV_ROWS):
            r0 = t0 + r * CONV_ROWS
            conv_cols = []
            for c in range(w_b // LANES):
                cols = slice(c * LANES, (c + 1) * LANES)
                acc = None
                for rho in range(SUBLANES):
                    part = None
                    for k in range(CONV_K):
                        off = first_tap + k
                        if off % SUBLANES != rho:
                            continue
                        lo = r0 + off - rho
                        rows = CONV_ROWS + (SUBLANES if rho else 0)
                        term = gbuf[lo:lo + rows, cols] * cw_ref[k:k + 1, cols]
                        part = term if part is None else part + term
                    part = part[rho:rho + CONV_ROWS]
                    acc = part if acc is None else acc + part
                conv_cols.append(acc)
            conv_rows.append(jnp.concatenate(conv_cols, axis=1))
        conv = jnp.concatenate(conv_rows, axis=0) + cb_ref[...]
        y_b = _rms(_silu(_layer_norm(conv, cng_ref[...], cnb_ref[...])), gob_ref[...])

        y = jnp.concatenate([y_a, y_b], axis=1).astype(_BF16)
        o_ref[t0:t0 + MIX_SUB, :] = x + _rms(_dot(y, wmo_ref[...]), g_out)

    gbuf[0:CONV_HALO, :] = gbuf[tm:tm + CONV_HALO, :]


def _mixer(x2, ada3, g_pre, g_post, w_mix_in, lng, lnb, w_sp_pairs, b_sp_full, conv_w, conv_b,
           cng, cnb, goa, gob, w_mix_out, *, seq):
    m, d = x2.shape
    w_a = lng.shape[0]
    w_b = conv_w.shape[1]
    tm = MIX_TM
    tiles_per_seq = seq // tm
    row = lambda v: v.reshape(1, -1)
    return pl.pallas_call(
        functools.partial(_mixer_kernel, tiles_per_seq=tiles_per_seq),
        grid=(m // tm,),
        in_specs=[
            pl.BlockSpec((tm, d), lambda i: (i, 0)),
            pl.BlockSpec((1, N_MOD, d), lambda i: (i // tiles_per_seq, 0, 0)),
            _resident((1, d)),
            _resident((1, d)),
            _resident(w_mix_in.shape),
            _resident((1, w_a)),
            _resident((1, w_a)),
            _resident(w_sp_pairs.shape),
            _resident(b_sp_full.shape),
            _resident(conv_w.shape),
            _resident((1, w_b)),
            _resident((1, w_b)),
            _resident((1, w_b)),
            _resident((1, w_a)),
            _resident((1, w_b)),
            _resident(w_mix_out.shape),
        ],
        out_specs=pl.BlockSpec((tm, d), lambda i: (i, 0)),
        out_shape=jax.ShapeDtypeStruct((m, d), _F32),
        scratch_shapes=[pltpu.VMEM((CONV_HALO + tm, w_b), _F32)],
        compiler_params=pltpu.CompilerParams(
            dimension_semantics=("arbitrary",), vmem_limit_bytes=VMEM_LIMIT),
        name="mixer",
    )(x2, ada3, row(g_pre), row(g_post), w_mix_in, row(lng), row(lnb), w_sp_pairs, b_sp_full,
      conv_w, row(conv_b), row(cng), row(cnb), row(goa), row(gob), w_mix_out)


def kernel(x, c, w_ada, b_ada, g_pre_f1, g_post_f1, w_f1_in, w_f1_out, g_pre_m, g_post_m, w_mix_in,
           gmlp_norm_g, gmlp_norm_b, w_spatial, b_spatial, conv_w, conv_b, conv_norm_g, conv_norm_b,
           g_out_a, g_out_b, w_mix_out, g_pre_f2, g_post_f2, w_f2_in, w_f2_out):
    bsz, seq, d = x.shape
    depth = w_ada.shape[0]
    w_a = gmlp_norm_g.shape[1]
    hd_a = w_a // H_A
    assert seq % FFN_TM == 0 and FFN_TM % FFN_SUB == 0
    assert seq % MIX_TM == 0 and MIX_TM % MIX_SUB == 0 and MIX_SUB % CHUNK == 0
    assert 2 * hd_a == LANES and w_spatial.shape[2:] == (CHUNK, CHUNK)
    assert CONV_HALO >= CONV_K - 1 and MIX_SUB % CONV_ROWS == 0

    x2 = x.reshape(bsz * seq, d)
    for l in range(depth):
        ada3 = _ada(c, w_ada[l], b_ada[l]).reshape(bsz, N_MOD, d)
        x2 = _ffn(x2, ada3, g_pre_f1[l], g_post_f1[l], w_f1_in[l].astype(_BF16),
                  w_f1_out[l].astype(_BF16), mod_row=0, seq=seq)
        w_sp_pairs = w_spatial[l].reshape(H_A // 2, 2, CHUNK, CHUNK).transpose(0, 2, 1, 3)
        w_sp_pairs = w_sp_pairs.reshape(H_A // 2, CHUNK, 2 * CHUNK)
        b_sp_full = jnp.repeat(b_spatial[l].T, hd_a, axis=1)
        x2 = _mixer(x2, ada3, g_pre_m[l], g_post_m[l], w_mix_in[l].astype(_BF16), gmlp_norm_g[l],
                    gmlp_norm_b[l], w_sp_pairs, b_sp_full, conv_w[l], conv_b[l], conv_norm_g[l],
                    conv_norm_b[l], g_out_a[l], g_out_b[l], w_mix_out[l].astype(_BF16), seq=seq)
        x2 = _ffn(x2, ada3, g_pre_f2[l], g_post_f2[l], w_f2_in[l].astype(_BF16),
                  w_f2_out[l].astype(_BF16), mod_row=6, seq=seq)
    return x2.reshape(bsz, seq, d)
```

```python
import functools

import jax
import jax.numpy as jnp
from jax import lax
from jax.experimental import pallas as pl
from jax.experimental.pallas import tpu as pltpu

N_MOD = 9
H_A = 8
CHUNK = 128
CONV_K = 31
EPS = 1e-6
HALF = 0.5

CONV_HALO = 32
CONV_ROWS = 64
LANES = 128
SUBLANES = 8
MXU_N = 256

FFN_TM = 512
FFN_SUB = 256
MIX_TM = 512
MIX_SUB = 256
ADA_TN = 1024
VMEM_LIMIT = 56 * 1024 * 1024

_F32 = jnp.float32
_BF16 = jnp.bfloat16


def _rms(x, g):
    return x * lax.rsqrt(jnp.mean(x * x, axis=-1, keepdims=True) + EPS) * g


def _layer_norm(x, g, b):
    mu = jnp.mean(x, axis=-1, keepdims=True)
    xc = x - mu
    var = jnp.mean(xc * xc, axis=-1, keepdims=True)
    return xc * lax.rsqrt(var + EPS) * g + b


def _silu(x):
    return x * jax.nn.sigmoid(x)


def _dot(a, b):
    return jnp.dot(a, b, preferred_element_type=_F32)


def _ada_kernel(c_ref, w_ref, b_ref, o_ref):
    s = _silu(c_ref[...]).astype(_BF16)
    o_ref[...] = _dot(s, w_ref[...].astype(_BF16)) + b_ref[...]


def _ada(c, w, b):
    bsz, d = c.shape
    n = w.shape[1]
    return pl.pallas_call(
        _ada_kernel,
        grid=(n // ADA_TN,),
        in_specs=[
            pl.BlockSpec((bsz, d), lambda j: (0, 0)),
            pl.BlockSpec((d, ADA_TN), lambda j: (0, j)),
            pl.BlockSpec((1, ADA_TN), lambda j: (0, j)),
        ],
        out_specs=pl.BlockSpec((bsz, ADA_TN), lambda j: (0, j)),
        out_shape=jax.ShapeDtypeStruct((bsz, n), _F32),
        compiler_params=pltpu.CompilerParams(
            dimension_semantics=("parallel",), vmem_limit_bytes=VMEM_LIMIT),
        name="ada_proj",
    )(c, w, b.reshape(1, n))


def _ffn_kernel(x_ref, ada_ref, gpre_ref, gpost_ref, win_ref, wout_ref, o_ref, *, mod_row, d_ff):
    ada = ada_ref[0]
    shift = ada[mod_row:mod_row + 1]
    scale = ada[mod_row + 1:mod_row + 2]
    gate = ada[mod_row + 2:mod_row + 3]
    g_in = gpre_ref[...] * (1 + scale)
    g_out = (HALF * gate) * gpost_ref[...]
    for s in range(x_ref.shape[0] // FFN_SUB):
        rows = slice(s * FFN_SUB, (s + 1) * FFN_SUB)
        x = x_ref[rows, :]
        h = (_rms(x, g_in) + shift).astype(_BF16)
        gu = _dot(h, win_ref[...])
        tiles = [gu[:, j * MXU_N:(j + 1) * MXU_N] for j in range(2 * d_ff // MXU_N)]
        act = jnp.concatenate([_silu(g) * u for g, u in zip(tiles[0::2], tiles[1::2])], axis=1).astype(_BF16)
        y = _dot(act, wout_ref[...])
        o_ref[rows, :] = x + _rms(y, g_out)


def _resident(shape):
    return pl.BlockSpec(shape, lambda i: (0,) * len(shape), pipeline_mode=pl.Buffered(1))


def _gate_up_tiles(w_in):
    d = w_in.shape[0]
    return w_in.reshape(d, 2, -1, MXU_N).transpose(0, 2, 1, 3).reshape(d, -1).astype(_BF16)


def _ffn(x2, ada3, g_pre, g_post, w_in, w_out, *, mod_row, seq):
    m, d = x2.shape
    d_ff = w_out.shape[0]
    tm = FFN_TM
    tiles_per_seq = seq // tm
    return pl.pallas_call(
        functools.partial(_ffn_kernel, mod_row=mod_row, d_ff=d_ff),
        grid=(m // tm,),
        in_specs=[
            pl.BlockSpec((tm, d), lambda i: (i, 0)),
            pl.BlockSpec((1, N_MOD, d), lambda i: (i // tiles_per_seq, 0, 0)),
            _resident((1, d)),
            _resident((1, d)),
            _resident((d, 2 * d_ff)),
            _resident((d_ff, d)),
        ],
        out_specs=pl.BlockSpec((tm, d), lambda i: (i, 0)),
        out_shape=jax.ShapeDtypeStruct((m, d), _F32),
        compiler_params=pltpu.CompilerParams(
            dimension_semantics=("parallel",), vmem_limit_bytes=VMEM_LIMIT),
        name=f"ffn_mod{mod_row}",
    )(x2, ada3, g_pre.reshape(1, d), g_post.reshape(1, d), w_in, w_out)


def _mixer_kernel(x_ref, ada_ref, gpre_ref, gpost_ref, wmi_ref, lng_ref, lnb_ref, wsp_ref, bsp_ref,
                  cw_ref, cb_ref, cng_ref, cnb_ref, goa_ref, gob_ref, wmo_ref, o_ref, gbuf,
                  *, tiles_per_seq):
    tm, d = x_ref.shape
    w_a = lng_ref.shape[1]
    w_b = cw_ref.shape[1]
    n_pairs = w_a // LANES

    @pl.when(pl.program_id(0) % tiles_per_seq == 0)
    def _():
        gbuf[0:CONV_HALO, :] = jnp.zeros((CONV_HALO, w_b), _F32)

    ada = ada_ref[0]
    shift, scale, gate = ada[3:4], ada[4:5], ada[5:6]
    g_in = gpre_ref[...] * (1 + scale)
    g_out = gate * gpost_ref[...]

    t_idx = lax.broadcasted_iota(jnp.int32, (CHUNK, 2 * CHUNK), 0)
    s_idx = lax.broadcasted_iota(jnp.int32, (CHUNK, 2 * CHUNK), 1) % CHUNK
    left = lax.broadcasted_iota(jnp.int32, (CHUNK, LANES), 1) < (LANES // 2)
    w_pairs = [jnp.where(s_idx <= t_idx, wsp_ref[p], 0.0).astype(_BF16) for p in range(n_pairs)]
    first_tap = CONV_HALO - (CONV_K - 1)

    def project(t0):
        x = x_ref[t0:t0 + MIX_SUB, :]
        h = (_rms(x, g_in) + shift).astype(_BF16)
        return x, _dot(h, wmi_ref[...])

    n_sub = tm // MIX_SUB
    nxt = project(0)
    for sub in range(n_sub):
        t0 = sub * MIX_SUB
        x, proj = nxt
        if sub + 1 < n_sub:
            nxt = project(t0 + MIX_SUB)
        ag = [proj[:, j * MXU_N:(j + 1) * MXU_N] for j in range(2 * w_b // MXU_N)]
        a = jnp.concatenate(ag[0::2], axis=1)
        g = jnp.concatenate(ag[1::2], axis=1)
        v = proj[:, 2 * w_b:2 * w_b + w_a]
        u = proj[:, 2 * w_b + w_a:]

        gbuf[CONV_HALO + t0:CONV_HALO + t0 + MIX_SUB, :] = a * jax.nn.sigmoid(g)
        conv_rows = []
        for r in range(MIX_SUB // CONV_ROWS):
            r0 = t0 + r * CONV_ROWS
            conv_cols = []
            for c in range(w_b // LANES):
                cols = slice(c * LANES, (c + 1) * LANES)
                acc = None
                for rho in range(SUBLANES):
                    part = None
                    for k in range(CONV_K):
                        off = first_tap + k
                        if off % SUBLANES != rho:
                            continue
                        lo = r0 + off - rho
                        rows = CONV_ROWS + (SUBLANES if rho else 0)
                        term = gbuf[lo:lo + rows, cols] * cw_ref[k:k + 1, cols]
                        part = term if part is None else part + term
                    part = part[rho:rho + CONV_ROWS]
                    acc = part if acc is None else acc + part
                conv_cols.append(acc)
            conv_rows.append(jnp.concatenate(conv_cols, axis=1))
        conv = jnp.concatenate(conv_rows, axis=0) + cb_ref[...]
        y_b = _rms(_silu(_layer_norm(conv, cng_ref[...], cnb_ref[...])), gob_ref[...])

        v = _layer_norm(v, lng_ref[...], lnb_ref[...])
        z_rows = []
        for n in range(MIX_SUB // CHUNK):
            z_cols = []
            for p in range(n_pairs):
                vp = v[n * CHUNK:(n + 1) * CHUNK, p * LANES:(p + 1) * LANES]
                rhs = jnp.concatenate([jnp.where(left, vp, 0.0), jnp.where(left, 0.0, vp)], axis=0)
                z_cols.append(_dot(w_pairs[p], rhs.astype(_BF16)))
            z_rows.append(jnp.concatenate(z_cols, axis=1) + bsp_ref[...])
        y_a = _rms(u * jnp.concatenate(z_rows, axis=0), goa_ref[...])

        y = jnp.concatenate([y_a, y_b], axis=1).astype(_BF16)
        o_ref[t0:t0 + MIX_SUB, :] = x + _rms(_dot(y, wmo_ref[...]), g_out)

    gbuf[0:CONV_HALO, :] = gbuf[tm:tm + CONV_HALO, :]


def _mixer(x2, ada3, g_pre, g_post, w_mix_in, lng, lnb, w_sp_pairs, b_sp_full, conv_w, conv_b,
           cng, cnb, goa, gob, w_mix_out, *, seq):
    m, d = x2.shape
    w_a = lng.shape[0]
    w_b = conv_w.shape[1]
    tm = MIX_TM
    tiles_per_seq = seq // tm
    row = lambda v: v.reshape(1, -1)
    return pl.pallas_call(
        functools.partial(_mixer_kernel, tiles_per_seq=tiles_per_seq),
        grid=(m // tm,),
        in_specs=[
            pl.BlockSpec((tm, d), lambda i: (i, 0)),
            pl.BlockSpec((1, N_MOD, d), lambda i: (i // tiles_per_seq, 0, 0)),
            _resident((1, d)),
            _resident((1, d)),
            _resident(w_mix_in.shape),
            _resident((1, w_a)),
            _resident((1, w_a)),
            _resident(w_sp_pairs.shape),
            _resident(b_sp_full.shape),
            _resident(conv_w.shape),
            _resident((1, w_b)),
            _resident((1, w_b)),
            _resident((1, w_b)),
            _resident((1, w_a)),
            _resident((1, w_b)),
            _resident(w_mix_out.shape),
        ],
        out_specs=pl.BlockSpec((tm, d), lambda i: (i, 0)),
        out_shape=jax.ShapeDtypeStruct((m, d), _F32),
        scratch_shapes=[pltpu.VMEM((CONV_HALO + tm, w_b), _F32)],
        compiler_params=pltpu.CompilerParams(
            dimension_semantics=("arbitrary",), vmem_limit_bytes=VMEM_LIMIT),
        name="mixer",
    )(x2, ada3, row(g_pre), row(g_post), w_mix_in, row(lng), row(lnb), w_sp_pairs, b_sp_full,
      conv_w, row(conv_b), row(cng), row(cnb), row(goa), row(gob), w_mix_out)


def kernel(x, c, w_ada, b_ada, g_pre_f1, g_post_f1, w_f1_in, w_f1_out, g_pre_m, g_post_m, w_mix_in,
           gmlp_norm_g, gmlp_norm_b, w_spatial, b_spatial, conv_w, conv_b, conv_norm_g, conv_norm_b,
           g_out_a, g_out_b, w_mix_out, g_pre_f2, g_post_f2, w_f2_in, w_f2_out):
    bsz, seq, d = x.shape
    depth = w_ada.shape[0]
    w_a = gmlp_norm_g.shape[1]
    hd_a = w_a // H_A
    assert seq % FFN_TM == 0 and FFN_TM % FFN_SUB == 0
    assert seq % MIX_TM == 0 and MIX_TM % MIX_SUB == 0 and MIX_SUB % CHUNK == 0
    assert 2 * hd_a == LANES and w_spatial.shape[2:] == (CHUNK, CHUNK)
    assert CONV_HALO >= CONV_K - 1 and MIX_SUB % CONV_ROWS == 0

    x2 = x.reshape(bsz * seq, d)
    for l in range(depth):
        ada3 = _ada(c, w_ada[l], b_ada[l]).reshape(bsz, N_MOD, d)
        x2 = _ffn(x2, ada3, g_pre_f1[l], g_post_f1[l], _gate_up_tiles(w_f1_in[l]),
                  w_f1_out[l].astype(_BF16), mod_row=0, seq=seq)
        w_sp_pairs = w_spatial[l].reshape(H_A // 2, 2, CHUNK, CHUNK).transpose(0, 2, 1, 3)
        w_sp_pairs = w_sp_pairs.reshape(H_A // 2, CHUNK, 2 * CHUNK)
        b_sp_full = jnp.repeat(b_spatial[l].T, hd_a, axis=1)
        w_u, w_v, w_ag = w_mix_in[l][:, :w_a], w_mix_in[l][:, w_a:2 * w_a], w_mix_in[l][:, 2 * w_a:]
        w_ag = w_ag.reshape(d, 2, -1, MXU_N).transpose(0, 2, 1, 3).reshape(d, -1)
        w_in_m = jnp.concatenate([w_ag, w_v, w_u], axis=1).astype(_BF16)
        x2 = _mixer(x2, ada3, g_pre_m[l], g_post_m[l], w_in_m, gmlp_norm_g[l],
                    gmlp_norm_b[l], w_sp_pairs, b_sp_full, conv_w[l], conv_b[l], conv_norm_g[l],
                    conv_norm_b[l], g_out_a[l], g_out_b[l], w_mix_out[l].astype(_BF16), seq=seq)
        x2 = _ffn(x2, ada3, g_pre_f2[l], g_post_f2[l], _gate_up_tiles(w_f2_in[l]),
                  w_f2_out[l].astype(_BF16), mod_row=6, seq=seq)
    return x2.reshape(bsz, seq, d)
```

```python
import functools

import jax
import jax.numpy as jnp
from jax import lax
from jax.experimental import pallas as pl
from jax.experimental.pallas import tpu as pltpu

N_MOD = 9
H_A = 8
CHUNK = 128
CONV_K = 31
EPS = 1e-6
HALF = 0.5

CONV_HALO = 32
CONV_ROWS = 64
LANES = 128
SUBLANES = 8
MXU_N = 256

FFN_TM = 512
FFN_SUB = 256
MIX_TM = 512
MIX_SUB = 256
ADA_TN = 1024
VMEM_LIMIT = 56 * 1024 * 1024

_F32 = jnp.float32
_BF16 = jnp.bfloat16


def _rms(x, g):
    return x * lax.rsqrt(jnp.mean(x * x, axis=-1, keepdims=True) + EPS) * g


def _layer_norm(x, g, b):
    mu = jnp.mean(x, axis=-1, keepdims=True)
    xc = x - mu
    var = jnp.mean(xc * xc, axis=-1, keepdims=True)
    return xc * lax.rsqrt(var + EPS) * g + b


def _silu(x):
    return x * jax.nn.sigmoid(x)


def _dot(a, b):
    return jnp.dot(a, b, preferred_element_type=_F32)


def _ada_kernel(c_ref, w_ref, b_ref, o_ref):
    s = _silu(c_ref[...]).astype(_BF16)
    o_ref[...] = _dot(s, w_ref[...].astype(_BF16)) + b_ref[...]


def _ada(c, w, b):
    bsz, d = c.shape
    n = w.shape[1]
    return pl.pallas_call(
        _ada_kernel,
        grid=(n // ADA_TN,),
        in_specs=[
            pl.BlockSpec((bsz, d), lambda j: (0, 0)),
            pl.BlockSpec((d, ADA_TN), lambda j: (0, j)),
            pl.BlockSpec((1, ADA_TN), lambda j: (0, j)),
        ],
        out_specs=pl.BlockSpec((bsz, ADA_TN), lambda j: (0, j)),
        out_shape=jax.ShapeDtypeStruct((bsz, n), _F32),
        compiler_params=pltpu.CompilerParams(
            dimension_semantics=("parallel",), vmem_limit_bytes=VMEM_LIMIT),
        name="ada_proj",
    )(c, w, b.reshape(1, n))


def _ffn_kernel(x_ref, ada_ref, gpre_ref, gpost_ref, win_ref, wout_ref, o_ref, *, mod_row, d_ff):
    ada = ada_ref[0]
    shift = ada[mod_row:mod_row + 1]
    scale = ada[mod_row + 1:mod_row + 2]
    gate = ada[mod_row + 2:mod_row + 3]
    g_in = gpre_ref[...] * (1 + scale)
    g_out = (HALF * gate) * gpost_ref[...]
    for s in range(x_ref.shape[0] // FFN_SUB):
        rows = slice(s * FFN_SUB, (s + 1) * FFN_SUB)
        x = x_ref[rows, :]
        h = (_rms(x, g_in) + shift).astype(_BF16)
        act = []
        for j in range(d_ff // MXU_N):
            g = _dot(h, win_ref[:, j * MXU_N:(j + 1) * MXU_N])
            u = _dot(h, win_ref[:, d_ff + j * MXU_N:d_ff + (j + 1) * MXU_N])
            act.append(_silu(g) * u)
        act = jnp.concatenate(act, axis=1).astype(_BF16)
        y = _dot(act, wout_ref[...])
        o_ref[rows, :] = x + _rms(y, g_out)


def _resident(shape):
    return pl.BlockSpec(shape, lambda i: (0,) * len(shape), pipeline_mode=pl.Buffered(1))


def _ffn(x2, ada3, g_pre, g_post, w_in, w_out, *, mod_row, seq):
    m, d = x2.shape
    d_ff = w_out.shape[0]
    tm = FFN_TM
    tiles_per_seq = seq // tm
    return pl.pallas_call(
        functools.partial(_ffn_kernel, mod_row=mod_row, d_ff=d_ff),
        grid=(m // tm,),
        in_specs=[
            pl.BlockSpec((tm, d), lambda i: (i, 0)),
            pl.BlockSpec((1, N_MOD, d), lambda i: (i // tiles_per_seq, 0, 0)),
            _resident((1, d)),
            _resident((1, d)),
            _resident((d, 2 * d_ff)),
            _resident((d_ff, d)),
        ],
        out_specs=pl.BlockSpec((tm, d), lambda i: (i, 0)),
        out_shape=jax.ShapeDtypeStruct((m, d), _F32),
        compiler_params=pltpu.CompilerParams(
            dimension_semantics=("parallel",), vmem_limit_bytes=VMEM_LIMIT),
        name=f"ffn_mod{mod_row}",
    )(x2, ada3, g_pre.reshape(1, d), g_post.reshape(1, d), w_in, w_out)


def _mixer_kernel(x_ref, ada_ref, gpre_ref, gpost_ref, wmi_ref, lng_ref, lnb_ref, wsp_ref, bsp_ref,
                  cw_ref, cb_ref, cng_ref, cnb_ref, goa_ref, gob_ref, wmo_ref, o_ref, gbuf,
                  *, tiles_per_seq):
    tm, d = x_ref.shape
    w_a = lng_ref.shape[1]
    w_b = cw_ref.shape[1]
    n_pairs = w_a // LANES

    @pl.when(pl.program_id(0) % tiles_per_seq == 0)
    def _():
        gbuf[0:CONV_HALO, :] = jnp.zeros((CONV_HALO, w_b), _F32)

    ada = ada_ref[0]
    shift, scale, gate = ada[3:4], ada[4:5], ada[5:6]
    g_in = gpre_ref[...] * (1 + scale)
    g_out = gate * gpost_ref[...]

    t_idx = lax.broadcasted_iota(jnp.int32, (CHUNK, 2 * CHUNK), 0)
    s_idx = lax.broadcasted_iota(jnp.int32, (CHUNK, 2 * CHUNK), 1) % CHUNK
    left = lax.broadcasted_iota(jnp.int32, (CHUNK, LANES), 1) < (LANES // 2)
    w_pairs = [jnp.where(s_idx <= t_idx, wsp_ref[p], 0.0).astype(_BF16) for p in range(n_pairs)]
    first_tap = CONV_HALO - (CONV_K - 1)

    def project(t0):
        x = x_ref[t0:t0 + MIX_SUB, :]
        h = (_rms(x, g_in) + shift).astype(_BF16)
        tile = lambda c0: _dot(h, wmi_ref[:, c0:c0 + MXU_N])
        a_cols, g_cols = 2 * w_a, 2 * w_a + w_b
        ag = [(tile(a_cols + j * MXU_N), tile(g_cols + j * MXU_N)) for j in range(w_b // MXU_N)]
        v = _dot(h, wmi_ref[:, w_a:2 * w_a])
        u = _dot(h, wmi_ref[:, 0:w_a])
        a = jnp.concatenate([t[0] for t in ag], axis=1)
        g = jnp.concatenate([t[1] for t in ag], axis=1)
        return x, (u, v, a, g)

    n_sub = tm // MIX_SUB
    nxt = project(0)
    for sub in range(n_sub):
        t0 = sub * MIX_SUB
        x, proj = nxt
        if sub + 1 < n_sub:
            nxt = project(t0 + MIX_SUB)
        u, v, a, g = proj

        gbuf[CONV_HALO + t0:CONV_HALO + t0 + MIX_SUB, :] = a * jax.nn.sigmoid(g)
        conv_rows = []
        for r in range(MIX_SUB // CONV_ROWS):
            r0 = t0 + r * CONV_ROWS
            conv_cols = []
            for c in range(w_b // LANES):
                cols = slice(c * LANES, (c + 1) * LANES)
                acc = None
                for rho in range(SUBLANES):
                    part = None
                    for k in range(CONV_K):
                        off = first_tap + k
                        if off % SUBLANES != rho:
                            continue
                        lo = r0 + off - rho
                        rows = CONV_ROWS + (SUBLANES if rho else 0)
                        term = gbuf[lo:lo + rows, cols] * cw_ref[k:k + 1, cols]
                        part = term if part is None else part + term
                    part = part[rho:rho + CONV_ROWS]
                    acc = part if acc is None else acc + part
                conv_cols.append(acc)
            conv_rows.append(jnp.concatenate(conv_cols, axis=1))
        conv = jnp.concatenate(conv_rows, axis=0) + cb_ref[...]
        y_b = _rms(_silu(_layer_norm(conv, cng_ref[...], cnb_ref[...])), gob_ref[...])

        v = _layer_norm(v, lng_ref[...], lnb_ref[...])
        z_rows = []
        for n in range(MIX_SUB // CHUNK):
            z_cols = []
            for p in range(n_pairs):
                vp = v[n * CHUNK:(n + 1) * CHUNK, p * LANES:(p + 1) * LANES]
                rhs = jnp.concatenate([jnp.where(left, vp, 0.0), jnp.where(left, 0.0, vp)], axis=0)
                z_cols.append(_dot(w_pairs[p], rhs.astype(_BF16)))
            z_rows.append(jnp.concatenate(z_cols, axis=1) + bsp_ref[...])
        y_a = _rms(u * jnp.concatenate(z_rows, axis=0), goa_ref[...])

        y = jnp.concatenate([y_a, y_b], axis=1).astype(_BF16)
        o_ref[t0:t0 + MIX_SUB, :] = x + _rms(_dot(y, wmo_ref[...]), g_out)

    gbuf[0:CONV_HALO, :] = gbuf[tm:tm + CONV_HALO, :]


def _mixer(x2, ada3, g_pre, g_post, w_mix_in, lng, lnb, w_sp_pairs, b_sp_full, conv_w, conv_b,
           cng, cnb, goa, gob, w_mix_out, *, seq):
    m, d = x2.shape
    w_a = lng.shape[0]
    w_b = conv_w.shape[1]
    tm = MIX_TM
    tiles_per_seq = seq // tm
    row = lambda v: v.reshape(1, -1)
    return pl.pallas_call(
        functools.partial(_mixer_kernel, tiles_per_seq=tiles_per_seq),
        grid=(m // tm,),
        in_specs=[
            pl.BlockSpec((tm, d), lambda i: (i, 0)),
            pl.BlockSpec((1, N_MOD, d), lambda i: (i // tiles_per_seq, 0, 0)),
            _resident((1, d)),
            _resident((1, d)),
            _resident(w_mix_in.shape),
            _resident((1, w_a)),
            _resident((1, w_a)),
            _resident(w_sp_pairs.shape),
            _resident(b_sp_full.shape),
            _resident(conv_w.shape),
            _resident((1, w_b)),
            _resident((1, w_b)),
            _resident((1, w_b)),
            _resident((1, w_a)),
            _resident((1, w_b)),
            _resident(w_mix_out.shape),
        ],
        out_specs=pl.BlockSpec((tm, d), lambda i: (i, 0)),
        out_shape=jax.ShapeDtypeStruct((m, d), _F32),
        scratch_shapes=[pltpu.VMEM((CONV_HALO + tm, w_b), _F32)],
        compiler_params=pltpu.CompilerParams(
            dimension_semantics=("arbitrary",), vmem_limit_bytes=VMEM_LIMIT),
        name="mixer",
    )(x2, ada3, row(g_pre), row(g_post), w_mix_in, row(lng), row(lnb), w_sp_pairs, b_sp_full,
      conv_w, row(conv_b), row(cng), row(cnb), row(goa), row(gob), w_mix_out)


def kernel(x, c, w_ada, b_ada, g_pre_f1, g_post_f1, w_f1_in, w_f1_out, g_pre_m, g_post_m, w_mix_in,
           gmlp_norm_g, gmlp_norm_b, w_spatial, b_spatial, conv_w, conv_b, conv_norm_g, conv_norm_b,
           g_out_a, g_out_b, w_mix_out, g_pre_f2, g_post_f2, w_f2_in, w_f2_out):
    bsz, seq, d = x.shape
    depth = w_ada.shape[0]
    w_a = gmlp_norm_g.shape[1]
    hd_a = w_a // H_A
    assert seq % FFN_TM == 0 and FFN_TM % FFN_SUB == 0
    assert seq % MIX_TM == 0 and MIX_TM % MIX_SUB == 0 and MIX_SUB % CHUNK == 0
    assert 2 * hd_a == LANES and w_spatial.shape[2:] == (CHUNK, CHUNK)
    assert CONV_HALO >= CONV_K - 1 and MIX_SUB % CONV_ROWS == 0

    x2 = x.reshape(bsz * seq, d)
    for l in range(depth):
        ada3 = _ada(c, w_ada[l], b_ada[l]).reshape(bsz, N_MOD, d)
        x2 = _ffn(x2, ada3, g_pre_f1[l], g_post_f1[l], w_f1_in[l].astype(_BF16),
                  w_f1_out[l].astype(_BF16), mod_row=0, seq=seq)
        w_sp_pairs = w_spatial[l].reshape(H_A // 2, 2, CHUNK, CHUNK).transpose(0, 2, 1, 3)
        w_sp_pairs = w_sp_pairs.reshape(H_A // 2, CHUNK, 2 * CHUNK)
        b_sp_full = jnp.repeat(b_spatial[l].T, hd_a, axis=1)
        x2 = _mixer(x2, ada3, g_pre_m[l], g_post_m[l], w_mix_in[l].astype(_BF16), gmlp_norm_g[l],
                    gmlp_norm_b[l], w_sp_pairs, b_sp_full, conv_w[l], conv_b[l], conv_norm_g[l],
                    conv_norm_b[l], g_out_a[l], g_out_b[l], w_mix_out[l].astype(_BF16), seq=seq)
        x2 = _ffn(x2, ada3, g_pre_f2[l], g_post_f2[l], w_f2_in[l].astype(_BF16),
                  w_f2_out[l].astype(_BF16), mod_row=6, seq=seq)
    return x2.reshape(bsz, seq, d)
```

```python
import functools

import jax
import jax.numpy as jnp
from jax import lax
from jax.experimental import pallas as pl
from jax.experimental.pallas import tpu as pltpu

N_MOD = 9
H_A = 8
CHUNK = 128
CONV_K = 31
EPS = 1e-6
HALF = 0.5

CONV_HALO = 32
CONV_ROWS = 64
LANES = 128
SUBLANES = 8
MXU_N = 256

FFN_TM = 1024
FFN_SUB = 256
MIX_TM = 512
MIX_SUB = 256
ADA_TN = 2304
VMEM_LIMIT = 56 * 1024 * 1024

_F32 = jnp.float32
_BF16 = jnp.bfloat16


def _rms(x, g):
    return x * lax.rsqrt(jnp.mean(x * x, axis=-1, keepdims=True) + EPS) * g


def _layer_norm(x, g, b):
    mu = jnp.mean(x, axis=-1, keepdims=True)
    xc = x - mu
    var = jnp.mean(xc * xc, axis=-1, keepdims=True)
    return xc * lax.rsqrt(var + EPS) * g + b


def _silu(x):
    return x * jax.nn.sigmoid(x)


def _dot(a, b):
    return jnp.dot(a, b, preferred_element_type=_F32)


def _ada_kernel(c_ref, w_ref, b_ref, o_ref):
    s = _silu(c_ref[...]).astype(_BF16)
    o_ref[...] = _dot(s, w_ref[...].astype(_BF16)) + b_ref[...]


def _ada(c, w, b):
    bsz, d = c.shape
    n = w.shape[1]
    return pl.pallas_call(
        _ada_kernel,
        grid=(n // ADA_TN,),
        in_specs=[
            pl.BlockSpec((bsz, d), lambda j: (0, 0)),
            pl.BlockSpec((d, ADA_TN), lambda j: (0, j)),
            pl.BlockSpec((1, ADA_TN), lambda j: (0, j)),
        ],
        out_specs=pl.BlockSpec((bsz, ADA_TN), lambda j: (0, j)),
        out_shape=jax.ShapeDtypeStruct((bsz, n), _F32),
        compiler_params=pltpu.CompilerParams(
            dimension_semantics=("parallel",), vmem_limit_bytes=VMEM_LIMIT),
        name="ada_proj",
    )(c, w, b.reshape(1, n))


def _ffn_kernel(x_ref, ada_ref, gpre_ref, gpost_ref, win_ref, wout_ref, o_ref, *, mod_row, d_ff):
    ada = ada_ref[0]
    shift = ada[mod_row:mod_row + 1]
    scale = ada[mod_row + 1:mod_row + 2]
    gate = ada[mod_row + 2:mod_row + 3]
    g_in = gpre_ref[...] * (1 + scale)
    g_out = (HALF * gate) * gpost_ref[...]
    for s in range(x_ref.shape[0] // FFN_SUB):
        rows = slice(s * FFN_SUB, (s + 1) * FFN_SUB)
        x = x_ref[rows, :]
        h = (_rms(x, g_in) + shift).astype(_BF16)
        act = []
        for j in range(d_ff // MXU_N):
            g = _dot(h, win_ref[:, j * MXU_N:(j + 1) * MXU_N])
            u = _dot(h, win_ref[:, d_ff + j * MXU_N:d_ff + (j + 1) * MXU_N])
            act.append(_silu(g) * u)
        act = jnp.concatenate(act, axis=1).astype(_BF16)
        y = _dot(act, wout_ref[...])
        o_ref[rows, :] = x + _rms(y, g_out)


def _resident(shape):
    return pl.BlockSpec(shape, lambda i: (0,) * len(shape), pipeline_mode=pl.Buffered(1))


def _ffn(x2, ada3, g_pre, g_post, w_in, w_out, *, mod_row, seq):
    m, d = x2.shape
    d_ff = w_out.shape[0]
    tm = FFN_TM
    tiles_per_seq = seq // tm
    return pl.pallas_call(
        functools.partial(_ffn_kernel, mod_row=mod_row, d_ff=d_ff),
        grid=(m // tm,),
        in_specs=[
            pl.BlockSpec((tm, d), lambda i: (i, 0)),
            pl.BlockSpec((1, N_MOD, d), lambda i: (i // tiles_per_seq, 0, 0)),
            _resident((1, d)),
            _resident((1, d)),
            _resident((d, 2 * d_ff)),
            _resident((d_ff, d)),
        ],
        out_specs=pl.BlockSpec((tm, d), lambda i: (i, 0)),
        out_shape=jax.ShapeDtypeStruct((m, d), _F32),
        compiler_params=pltpu.CompilerParams(
            dimension_semantics=("parallel",), vmem_limit_bytes=VMEM_LIMIT),
        name=f"ffn_mod{mod_row}",
    )(x2, ada3, g_pre.reshape(1, d), g_post.reshape(1, d), w_in, w_out)


def _mixer_kernel(x_ref, ada_ref, gpre_ref, gpost_ref, wmi_ref, lng_ref, lnb_ref, wsp_ref, bsp_ref,
                  cw_ref, cb_ref, cng_ref, cnb_ref, goa_ref, gob_ref, wmo_ref, o_ref, gbuf,
                  *, tiles_per_seq):
    tm, d = x_ref.shape
    w_a = lng_ref.shape[1]
    w_b = cw_ref.shape[1]
    n_pairs = w_a // LANES

    @pl.when(pl.program_id(0) % tiles_per_seq == 0)
    def _():
        gbuf[0:CONV_HALO, :] = jnp.zeros((CONV_HALO, w_b), _F32)

    ada = ada_ref[0]
    shift, scale, gate = ada[3:4], ada[4:5], ada[5:6]
    g_in = gpre_ref[...] * (1 + scale)
    g_out = gate * gpost_ref[...]

    t_idx = lax.broadcasted_iota(jnp.int32, (CHUNK, 2 * CHUNK), 0)
    s_idx = lax.broadcasted_iota(jnp.int32, (CHUNK, 2 * CHUNK), 1) % CHUNK
    left = lax.broadcasted_iota(jnp.int32, (CHUNK, LANES), 1) < (LANES // 2)
    w_pairs = [jnp.where(s_idx <= t_idx, wsp_ref[p], 0.0).astype(_BF16) for p in range(n_pairs)]
    first_tap = CONV_HALO - (CONV_K - 1)

    def project(t0):
        x = x_ref[t0:t0 + MIX_SUB, :]
        h = (_rms(x, g_in) + shift).astype(_BF16)
        tile = lambda c0: _dot(h, wmi_ref[:, c0:c0 + MXU_N])
        a_cols, g_cols = 2 * w_a, 2 * w_a + w_b
        ag = [(tile(a_cols + j * MXU_N), tile(g_cols + j * MXU_N)) for j in range(w_b // MXU_N)]
        v = _dot(h, wmi_ref[:, w_a:2 * w_a])
        u = _dot(h, wmi_ref[:, 0:w_a])
        a = jnp.concatenate([t[0] for t in ag], axis=1)
        g = jnp.concatenate([t[1] for t in ag], axis=1)
        return x, (u, v, a, g)

    n_sub = tm // MIX_SUB
    nxt = project(0)
    for sub in range(n_sub):
        t0 = sub * MIX_SUB
        x, proj = nxt
        if sub + 1 < n_sub:
            nxt = project(t0 + MIX_SUB)
        u, v, a, g = proj

        gbuf[CONV_HALO + t0:CONV_HALO + t0 + MIX_SUB, :] = a * jax.nn.sigmoid(g)
        conv_rows = []
        for r in range(MIX_SUB // CONV_ROWS):
            r0 = t0 + r * CONV_ROWS
            conv_cols = []
            for c in range(w_b // LANES):
                cols = slice(c * LANES, (c + 1) * LANES)
                acc = None
                for rho in range(SUBLANES):
                    part = None
                    for k in range(CONV_K):
                        off = first_tap + k
                        if off % SUBLANES != rho:
                            continue
                        lo = r0 + off - rho
                        rows = CONV_ROWS + (SUBLANES if rho else 0)
                        term = gbuf[lo:lo + rows, cols] * cw_ref[k:k + 1, cols]
                        part = term if part is None else part + term
                    part = part[rho:rho + CONV_ROWS]
                    acc = part if acc is None else acc + part
                conv_cols.append(acc)
            conv_rows.append(jnp.concatenate(conv_cols, axis=1))
        conv = jnp.concatenate(conv_rows, axis=0) + cb_ref[...]
        y_b = _rms(_silu(_layer_norm(conv, cng_ref[...], cnb_ref[...])), gob_ref[...])

        v = _layer_norm(v, lng_ref[...], lnb_ref[...])
        z_rows = []
        for n in range(MIX_SUB // CHUNK):
            z_cols = []
            for p in range(n_pairs):
                vp = v[n * CHUNK:(n + 1) * CHUNK, p * LANES:(p + 1) * LANES]
                rhs = jnp.concatenate([jnp.where(left, vp, 0.0), jnp.where(left, 0.0, vp)], axis=0)
                z_cols.append(_dot(w_pairs[p], rhs.astype(_BF16)))
            z_rows.append(jnp.concatenate(z_cols, axis=1) + bsp_ref[...])
        y_a = _rms(u * jnp.concatenate(z_rows, axis=0), goa_ref[...])

        y = jnp.concatenate([y_a, y_b], axis=1).astype(_BF16)
        o_ref[t0:t0 + MIX_SUB, :] = x + _rms(_dot(y, wmo_ref[...]), g_out)

    gbuf[0:CONV_HALO, :] = gbuf[tm:tm + CONV_HALO, :]


def _mixer(x2, ada3, g_pre, g_post, w_mix_in, lng, lnb, w_sp_pairs, b_sp_full, conv_w, conv_b,
           cng, cnb, goa, gob, w_mix_out, *, seq):
    m, d = x2.shape
    w_a = lng.shape[0]
    w_b = conv_w.shape[1]
    tm = MIX_TM
    tiles_per_seq = seq // tm
    row = lambda v: v.reshape(1, -1)
    return pl.pallas_call(
        functools.partial(_mixer_kernel, tiles_per_seq=tiles_per_seq),
        grid=(m // tm,),
        in_specs=[
            pl.BlockSpec((tm, d), lambda i: (i, 0)),
            pl.BlockSpec((1, N_MOD, d), lambda i: (i // tiles_per_seq, 0, 0)),
            _resident((1, d)),
            _resident((1, d)),
            _resident(w_mix_in.shape),
            _resident((1, w_a)),
            _resident((1, w_a)),
            _resident(w_sp_pairs.shape),
            _resident(b_sp_full.shape),
            _resident(conv_w.shape),
            _resident((1, w_b)),
            _resident((1, w_b)),
            _resident((1, w_b)),
            _resident((1, w_a)),
            _resident((1, w_b)),
            _resident(w_mix_out.shape),
        ],
        out_specs=pl.BlockSpec((tm, d), lambda i: (i, 0)),
        out_shape=jax.ShapeDtypeStruct((m, d), _F32),
        scratch_shapes=[pltpu.VMEM((CONV_HALO + tm, w_b), _F32)],
        compiler_params=pltpu.CompilerParams(
            dimension_semantics=("arbitrary",), vmem_limit_bytes=VMEM_LIMIT),
        name="mixer",
    )(x2, ada3, row(g_pre), row(g_post), w_mix_in, row(lng), row(lnb), w_sp_pairs, b_sp_full,
      conv_w, row(conv_b), row(cng), row(cnb), row(goa), row(gob), w_mix_out)


def kernel(x, c, w_ada, b_ada, g_pre_f1, g_post_f1, w_f1_in, w_f1_out, g_pre_m, g_post_m, w_mix_in,
           gmlp_norm_g, gmlp_norm_b, w_spatial, b_spatial, conv_w, conv_b, conv_norm_g, conv_norm_b,
           g_out_a, g_out_b, w_mix_out, g_pre_f2, g_post_f2, w_f2_in, w_f2_out):
    bsz, seq, d = x.shape
    depth = w_ada.shape[0]
    w_a = gmlp_norm_g.shape[1]
    hd_a = w_a // H_A
    assert seq % FFN_TM == 0 and FFN_TM % FFN_SUB == 0
    assert seq % MIX_TM == 0 and MIX_TM % MIX_SUB == 0 and MIX_SUB % CHUNK == 0
    assert 2 * hd_a == LANES and w_spatial.shape[2:] == (CHUNK, CHUNK)
    assert CONV_HALO >= CONV_K - 1 and MIX_SUB % CONV_ROWS == 0

    x2 = x.reshape(bsz * seq, d)
    for l in range(depth):
        ada3 = _ada(c, w_ada[l], b_ada[l]).reshape(bsz, N_MOD, d)
        x2 = _ffn(x2, ada3, g_pre_f1[l], g_post_f1[l], w_f1_in[l].astype(_BF16),
                  w_f1_out[l].astype(_BF16), mod_row=0, seq=seq)
        w_sp_pairs = w_spatial[l].reshape(H_A // 2, 2, CHUNK, CHUNK).transpose(0, 2, 1, 3)
        w_sp_pairs = w_sp_pairs.reshape(H_A // 2, CHUNK, 2 * CHUNK)
        b_sp_full = jnp.repeat(b_spatial[l].T, hd_a, axis=1)
        x2 = _mixer(x2, ada3, g_pre_m[l], g_post_m[l], w_mix_in[l].astype(_BF16), gmlp_norm_g[l],
                    gmlp_norm_b[l], w_sp_pairs, b_sp_full, conv_w[l], conv_b[l], conv_norm_g[l],
                    conv_norm_b[l], g_out_a[l], g_out_b[l], w_mix_out[l].astype(_BF16), seq=seq)
        x2 = _ffn(x2, ada3, g_pre_f2[l], g_post_f2[l], w_f2_in[l].astype(_BF16),
                  w_f2_out[l].astype(_BF16), mod_row=6, seq=seq)
    return x2.reshape(bsz, seq, d)
```

```python
import functools

import jax
import jax.numpy as jnp
from jax import lax
from jax.experimental import pallas as pl
from jax.experimental.pallas import tpu as pltpu

N_MOD = 9
H_A = 8
CHUNK = 128
CONV_K = 31
EPS = 1e-6
HALF = 0.5

CONV_HALO = 32
CONV_ROWS = 64
LANES = 128
SUBLANES = 8
MXU_N = 256

FFN_TM = 1024
FFN_SUB = 256
MIX_TM = 512
MIX_SUB = 256
ADA_TN = 2304
W_CHUNKS = 16
VMEM_LIMIT = 56 * 1024 * 1024

_F32 = jnp.float32
_BF16 = jnp.bfloat16


def _rms(x, g):
    return x * lax.rsqrt(jnp.mean(x * x, axis=-1, keepdims=True) + EPS) * g


def _layer_norm(x, g, b):
    mu = jnp.mean(x, axis=-1, keepdims=True)
    xc = x - mu
    var = jnp.mean(xc * xc, axis=-1, keepdims=True)
    return xc * lax.rsqrt(var + EPS) * g + b


def _silu(x):
    return x * jax.nn.sigmoid(x)


def _dot(a, b):
    return jnp.dot(a, b, preferred_element_type=_F32)


def _ada_kernel(c_ref, w_ref, b_ref, o_ref):
    s = _silu(c_ref[...]).astype(_BF16)
    o_ref[...] = _dot(s, w_ref[...].astype(_BF16)) + b_ref[...]


def _ada(c, w, b):
    bsz, d = c.shape
    n = w.shape[1]
    return pl.pallas_call(
        _ada_kernel,
        grid=(n // ADA_TN,),
        in_specs=[
            pl.BlockSpec((bsz, d), lambda j: (0, 0)),
            pl.BlockSpec((d, ADA_TN), lambda j: (0, j)),
            pl.BlockSpec((1, ADA_TN), lambda j: (0, j)),
        ],
        out_specs=pl.BlockSpec((bsz, ADA_TN), lambda j: (0, j)),
        out_shape=jax.ShapeDtypeStruct((bsz, n), _F32),
        compiler_params=pltpu.CompilerParams(
            dimension_semantics=("parallel",), vmem_limit_bytes=VMEM_LIMIT),
        name="ada_proj",
    )(c, w, b.reshape(1, n))


def _stage_bf16(src_hbm, dst_ref, stage_ref, sems):
    rows = stage_ref.shape[1]
    n_chunks = src_hbm.shape[0] // rows

    def copy(i):
        return pltpu.make_async_copy(src_hbm.at[pl.ds(i * rows, rows), :], stage_ref.at[i % 2], sems.at[i % 2])

    copy(0).start()
    for i in range(n_chunks):
        if i + 1 < n_chunks:
            copy(i + 1).start()
        copy(i).wait()
        dst_ref[i * rows:(i + 1) * rows, :] = stage_ref[i % 2].astype(_BF16)


def _ffn_kernel(x_ref, ada_ref, gpre_ref, gpost_ref, win_hbm, wout_hbm, o_ref,
                win_ref, wout_ref, stage_in, stage_out, sem_in, sem_out, *, mod_row, d_ff):
    @pl.when(pl.program_id(0) == 0)
    def _():
        _stage_bf16(win_hbm, win_ref, stage_in, sem_in)
        _stage_bf16(wout_hbm, wout_ref, stage_out, sem_out)

    ada = ada_ref[0]
    shift = ada[mod_row:mod_row + 1]
    scale = ada[mod_row + 1:mod_row + 2]
    gate = ada[mod_row + 2:mod_row + 3]
    g_in = gpre_ref[...] * (1 + scale)
    g_out = (HALF * gate) * gpost_ref[...]
    for s in range(x_ref.shape[0] // FFN_SUB):
        rows = slice(s * FFN_SUB, (s + 1) * FFN_SUB)
        x = x_ref[rows, :]
        h = (_rms(x, g_in) + shift).astype(_BF16)
        act = []
        for j in range(d_ff // MXU_N):
            g = _dot(h, win_ref[:, j * MXU_N:(j + 1) * MXU_N])
            u = _dot(h, win_ref[:, d_ff + j * MXU_N:d_ff + (j + 1) * MXU_N])
            act.append(_silu(g) * u)
        act = jnp.concatenate(act, axis=1).astype(_BF16)
        y = _dot(act, wout_ref[...])
        o_ref[rows, :] = x + _rms(y, g_out)


def _resident(shape):
    return pl.BlockSpec(shape, lambda i: (0,) * len(shape), pipeline_mode=pl.Buffered(1))


def _ffn(x2, ada3, g_pre, g_post, w_in, w_out, *, mod_row, seq):
    m, d = x2.shape
    d_ff = w_out.shape[0]
    tm = FFN_TM
    tiles_per_seq = seq // tm
    return pl.pallas_call(
        functools.partial(_ffn_kernel, mod_row=mod_row, d_ff=d_ff),
        grid=(m // tm,),
        in_specs=[
            pl.BlockSpec((tm, d), lambda i: (i, 0)),
            pl.BlockSpec((1, N_MOD, d), lambda i: (i // tiles_per_seq, 0, 0)),
            _resident((1, d)),
            _resident((1, d)),
            pl.BlockSpec(memory_space=pl.ANY),
            pl.BlockSpec(memory_space=pl.ANY),
        ],
        out_specs=pl.BlockSpec((tm, d), lambda i: (i, 0)),
        out_shape=jax.ShapeDtypeStruct((m, d), _F32),
        scratch_shapes=[
            pltpu.VMEM((d, 2 * d_ff), _BF16),
            pltpu.VMEM((d_ff, d), _BF16),
            pltpu.VMEM((2, d // W_CHUNKS, 2 * d_ff), _F32),
            pltpu.VMEM((2, d_ff // W_CHUNKS, d), _F32),
            pltpu.SemaphoreType.DMA((2,)),
            pltpu.SemaphoreType.DMA((2,)),
        ],
        compiler_params=pltpu.CompilerParams(
            dimension_semantics=("arbitrary",), vmem_limit_bytes=VMEM_LIMIT),
        name=f"ffn_mod{mod_row}",
    )(x2, ada3, g_pre.reshape(1, d), g_post.reshape(1, d), w_in, w_out)


def _mixer_kernel(x_ref, ada_ref, gpre_ref, gpost_ref, wmi_ref, lng_ref, lnb_ref, wsp_ref, bsp_ref,
                  cw_ref, cb_ref, cng_ref, cnb_ref, goa_ref, gob_ref, wmo_ref, o_ref, gbuf,
                  *, tiles_per_seq):
    tm, d = x_ref.shape
    w_a = lng_ref.shape[1]
    w_b = cw_ref.shape[1]
    n_pairs = w_a // LANES

    @pl.when(pl.program_id(0) % tiles_per_seq == 0)
    def _():
        gbuf[0:CONV_HALO, :] = jnp.zeros((CONV_HALO, w_b), _F32)

    ada = ada_ref[0]
    shift, scale, gate = ada[3:4], ada[4:5], ada[5:6]
    g_in = gpre_ref[...] * (1 + scale)
    g_out = gate * gpost_ref[...]

    t_idx = lax.broadcasted_iota(jnp.int32, (CHUNK, 2 * CHUNK), 0)
    s_idx = lax.broadcasted_iota(jnp.int32, (CHUNK, 2 * CHUNK), 1) % CHUNK
    left = lax.broadcasted_iota(jnp.int32, (CHUNK, LANES), 1) < (LANES // 2)
    w_pairs = [jnp.where(s_idx <= t_idx, wsp_ref[p], 0.0).astype(_BF16) for p in range(n_pairs)]
    first_tap = CONV_HALO - (CONV_K - 1)

    def project(t0):
        x = x_ref[t0:t0 + MIX_SUB, :]
        h = (_rms(x, g_in) + shift).astype(_BF16)
        tile = lambda c0: _dot(h, wmi_ref[:, c0:c0 + MXU_N])
        a_cols, g_cols = 2 * w_a, 2 * w_a + w_b
        ag = [(tile(a_cols + j * MXU_N), tile(g_cols + j * MXU_N)) for j in range(w_b // MXU_N)]
        v = _dot(h, wmi_ref[:, w_a:2 * w_a])
        u = _dot(h, wmi_ref[:, 0:w_a])
        a = jnp.concatenate([t[0] for t in ag], axis=1)
        g = jnp.concatenate([t[1] for t in ag], axis=1)
        return x, (u, v, a, g)

    n_sub = tm // MIX_SUB
    nxt = project(0)
    for sub in range(n_sub):
        t0 = sub * MIX_SUB
        x, proj = nxt
        if sub + 1 < n_sub:
            nxt = project(t0 + MIX_SUB)
        u, v, a, g = proj

        gbuf[CONV_HALO + t0:CONV_HALO + t0 + MIX_SUB, :] = a * jax.nn.sigmoid(g)
        conv_rows = []
        for r in range(MIX_SUB // CONV_ROWS):
            r0 = t0 + r * CONV_ROWS
            conv_cols = []
            for c in range(w_b // LANES):
                cols = slice(c * LANES, (c + 1) * LANES)
                acc = None
                for rho in range(SUBLANES):
                    part = None
                    for k in range(CONV_K):
                        off = first_tap + k
                        if off % SUBLANES != rho:
                            continue
                        lo = r0 + off - rho
                        rows = CONV_ROWS + (SUBLANES if rho else 0)
                        term = gbuf[lo:lo + rows, cols] * cw_ref[k:k + 1, cols]
                        part = term if part is None else part + term
                    part = part[rho:rho + CONV_ROWS]
                    acc = part if acc is None else acc + part
                conv_cols.append(acc)
            conv_rows.append(jnp.concatenate(conv_cols, axis=1))
        conv = jnp.concatenate(conv_rows, axis=0) + cb_ref[...]
        y_b = _rms(_silu(_layer_norm(conv, cng_ref[...], cnb_ref[...])), gob_ref[...])

        v = _layer_norm(v, lng_ref[...], lnb_ref[...])
        z_rows = []
        for n in range(MIX_SUB // CHUNK):
            z_cols = []
            for p in range(n_pairs):
                vp = v[n * CHUNK:(n + 1) * CHUNK, p * LANES:(p + 1) * LANES]
                rhs = jnp.concatenate([jnp.where(left, vp, 0.0), jnp.where(left, 0.0, vp)], axis=0)
                z_cols.append(_dot(w_pairs[p], rhs.astype(_BF16)))
            z_rows.append(jnp.concatenate(z_cols, axis=1) + bsp_ref[...])
        y_a = _rms(u * jnp.concatenate(z_rows, axis=0), goa_ref[...])

        y = jnp.concatenate([y_a, y_b], axis=1).astype(_BF16)
        o_ref[t0:t0 + MIX_SUB, :] = x + _rms(_dot(y, wmo_ref[...]), g_out)

    gbuf[0:CONV_HALO, :] = gbuf[tm:tm + CONV_HALO, :]


def _mixer(x2, ada3, g_pre, g_post, w_mix_in, lng, lnb, w_sp_pairs, b_sp_full, conv_w, conv_b,
           cng, cnb, goa, gob, w_mix_out, *, seq):
    m, d = x2.shape
    w_a = lng.shape[0]
    w_b = conv_w.shape[1]
    tm = MIX_TM
    tiles_per_seq = seq // tm
    row = lambda v: v.reshape(1, -1)
    return pl.pallas_call(
        functools.partial(_mixer_kernel, tiles_per_seq=tiles_per_seq),
        grid=(m // tm,),
        in_specs=[
            pl.BlockSpec((tm, d), lambda i: (i, 0)),
            pl.BlockSpec((1, N_MOD, d), lambda i: (i // tiles_per_seq, 0, 0)),
            _resident((1, d)),
            _resident((1, d)),
            _resident(w_mix_in.shape),
            _resident((1, w_a)),
            _resident((1, w_a)),
            _resident(w_sp_pairs.shape),
            _resident(b_sp_full.shape),
            _resident(conv_w.shape),
            _resident((1, w_b)),
            _resident((1, w_b)),
            _resident((1, w_b)),
            _resident((1, w_a)),
            _resident((1, w_b)),
            _resident(w_mix_out.shape),
        ],
        out_specs=pl.BlockSpec((tm, d), lambda i: (i, 0)),
        out_shape=jax.ShapeDtypeStruct((m, d), _F32),
        scratch_shapes=[pltpu.VMEM((CONV_HALO + tm, w_b), _F32)],
        compiler_params=pltpu.CompilerParams(
            dimension_semantics=("arbitrary",), vmem_limit_bytes=VMEM_LIMIT),
        name="mixer",
    )(x2, ada3, row(g_pre), row(g_post), w_mix_in, row(lng), row(lnb), w_sp_pairs, b_sp_full,
      conv_w, row(conv_b), row(cng), row(cnb), row(goa), row(gob), w_mix_out)


def kernel(x, c, w_ada, b_ada, g_pre_f1, g_post_f1, w_f1_in, w_f1_out, g_pre_m, g_post_m, w_mix_in,
           gmlp_norm_g, gmlp_norm_b, w_spatial, b_spatial, conv_w, conv_b, conv_norm_g, conv_norm_b,
           g_out_a, g_out_b, w_mix_out, g_pre_f2, g_post_f2, w_f2_in, w_f2_out):
    bsz, seq, d = x.shape
    depth = w_ada.shape[0]
    w_a = gmlp_norm_g.shape[1]
    hd_a = w_a // H_A
    assert seq % FFN_TM == 0 and FFN_TM % FFN_SUB == 0
    assert d % (W_CHUNKS * SUBLANES) == 0 and w_f1_out.shape[1] % (W_CHUNKS * SUBLANES) == 0
    assert seq % MIX_TM == 0 and MIX_TM % MIX_SUB == 0 and MIX_SUB % CHUNK == 0
    assert 2 * hd_a == LANES and w_spatial.shape[2:] == (CHUNK, CHUNK)
    assert CONV_HALO >= CONV_K - 1 and MIX_SUB % CONV_ROWS == 0

    x2 = x.reshape(bsz * seq, d)
    for l in range(depth):
        ada3 = _ada(c, w_ada[l], b_ada[l]).reshape(bsz, N_MOD, d)
        x2 = _ffn(x2, ada3, g_pre_f1[l], g_post_f1[l], w_f1_in[l], w_f1_out[l], mod_row=0, seq=seq)
        w_sp_pairs = w_spatial[l].reshape(H_A // 2, 2, CHUNK, CHUNK).transpose(0, 2, 1, 3)
        w_sp_pairs = w_sp_pairs.reshape(H_A // 2, CHUNK, 2 * CHUNK)
        b_sp_full = jnp.repeat(b_spatial[l].T, hd_a, axis=1)
        x2 = _mixer(x2, ada3, g_pre_m[l], g_post_m[l], w_mix_in[l].astype(_BF16), gmlp_norm_g[l],
                    gmlp_norm_b[l], w_sp_pairs, b_sp_full, conv_w[l], conv_b[l], conv_norm_g[l],
                    conv_norm_b[l], g_out_a[l], g_out_b[l], w_mix_out[l].astype(_BF16), seq=seq)
        x2 = _ffn(x2, ada3, g_pre_f2[l], g_post_f2[l], w_f2_in[l], w_f2_out[l], mod_row=6, seq=seq)
    return x2.reshape(bsz, seq, d)
```

```python
import functools

import jax
import jax.numpy as jnp
from jax import lax
from jax.experimental import pallas as pl
from jax.experimental.pallas import tpu as pltpu

N_MOD = 9
H_A = 8
CHUNK = 128
CONV_K = 31
EPS = 1e-6
HALF = 0.5

CONV_HALO = 32
CONV_ROWS = 64
LANES = 128
SUBLANES = 8
MXU_N = 256

FFN_TM = 1024
FFN_SUB = 256
MIX_TM = 512
MIX_SUB = 256
ADA_TN = 2304
W_CHUNKS = 16
W_SLOTS = 4
VMEM_LIMIT = 56 * 1024 * 1024

_F32 = jnp.float32
_BF16 = jnp.bfloat16


def _rms(x, g):
    return x * lax.rsqrt(jnp.mean(x * x, axis=-1, keepdims=True) + EPS) * g


def _layer_norm(x, g, b):
    mu = jnp.mean(x, axis=-1, keepdims=True)
    xc = x - mu
    var = jnp.mean(xc * xc, axis=-1, keepdims=True)
    return xc * lax.rsqrt(var + EPS) * g + b


def _silu(x):
    return x * jax.nn.sigmoid(x)


def _dot(a, b):
    return jnp.dot(a, b, preferred_element_type=_F32)


def _ada_kernel(c_ref, w_ref, b_ref, o_ref):
    s = _silu(c_ref[...]).astype(_BF16)
    o_ref[...] = _dot(s, w_ref[...].astype(_BF16)) + b_ref[...]


def _ada(c, w, b):
    bsz, d = c.shape
    n = w.shape[1]
    return pl.pallas_call(
        _ada_kernel,
        grid=(n // ADA_TN,),
        in_specs=[
            pl.BlockSpec((bsz, d), lambda j: (0, 0)),
            pl.BlockSpec((d, ADA_TN), lambda j: (0, j)),
            pl.BlockSpec((1, ADA_TN), lambda j: (0, j)),
        ],
        out_specs=pl.BlockSpec((bsz, ADA_TN), lambda j: (0, j)),
        out_shape=jax.ShapeDtypeStruct((bsz, n), _F32),
        compiler_params=pltpu.CompilerParams(
            dimension_semantics=("parallel",), vmem_limit_bytes=VMEM_LIMIT),
        name="ada_proj",
    )(c, w, b.reshape(1, n))


def _stage_bf16(src_hbm, dst_ref, stage_ref, sems):
    n_slots, rows, _ = stage_ref.shape
    n_chunks = src_hbm.shape[0] // rows

    def copy(i):
        slot = i % n_slots
        return pltpu.make_async_copy(src_hbm.at[pl.ds(i * rows, rows), :], stage_ref.at[slot], sems.at[slot])

    for i in range(min(n_slots - 1, n_chunks)):
        copy(i).start()
    for i in range(n_chunks):
        if i + n_slots - 1 < n_chunks:
            copy(i + n_slots - 1).start()
        copy(i).wait()
        dst_ref[i * rows:(i + 1) * rows, :] = stage_ref[i % n_slots].astype(_BF16)


def _ffn_kernel(x_ref, ada_ref, gpre_ref, gpost_ref, win_hbm, wout_hbm, o_ref,
                win_ref, wout_ref, stage_in, stage_out, sem_in, sem_out, *, mod_row, d_ff):
    @pl.when(pl.program_id(0) == 0)
    def _():
        _stage_bf16(win_hbm, win_ref, stage_in, sem_in)
        _stage_bf16(wout_hbm, wout_ref, stage_out, sem_out)

    ada = ada_ref[0]
    shift = ada[mod_row:mod_row + 1]
    scale = ada[mod_row + 1:mod_row + 2]
    gate = ada[mod_row + 2:mod_row + 3]
    g_in = gpre_ref[...] * (1 + scale)
    g_out = (HALF * gate) * gpost_ref[...]
    for s in range(x_ref.shape[0] // FFN_SUB):
        rows = slice(s * FFN_SUB, (s + 1) * FFN_SUB)
        x = x_ref[rows, :]
        h = (_rms(x, g_in) + shift).astype(_BF16)
        act = []
        for j in range(d_ff // MXU_N):
            g = _dot(h, win_ref[:, j * MXU_N:(j + 1) * MXU_N])
            u = _dot(h, win_ref[:, d_ff + j * MXU_N:d_ff + (j + 1) * MXU_N])
            act.append(_silu(g) * u)
        act = jnp.concatenate(act, axis=1).astype(_BF16)
        y = _dot(act, wout_ref[...])
        o_ref[rows, :] = x + _rms(y, g_out)


def _resident(shape):
    return pl.BlockSpec(shape, lambda i: (0,) * len(shape), pipeline_mode=pl.Buffered(1))


def _ffn(x2, ada3, g_pre, g_post, w_in, w_out, *, mod_row, seq):
    m, d = x2.shape
    d_ff = w_out.shape[0]
    tm = FFN_TM
    tiles_per_seq = seq // tm
    return pl.pallas_call(
        functools.partial(_ffn_kernel, mod_row=mod_row, d_ff=d_ff),
        grid=(m // tm,),
        in_specs=[
            pl.BlockSpec((tm, d), lambda i: (i, 0)),
            pl.BlockSpec((1, N_MOD, d), lambda i: (i // tiles_per_seq, 0, 0)),
            _resident((1, d)),
            _resident((1, d)),
            pl.BlockSpec(memory_space=pl.ANY),
            pl.BlockSpec(memory_space=pl.ANY),
        ],
        out_specs=pl.BlockSpec((tm, d), lambda i: (i, 0)),
        out_shape=jax.ShapeDtypeStruct((m, d), _F32),
        scratch_shapes=[
            pltpu.VMEM((d, 2 * d_ff), _BF16),
            pltpu.VMEM((d_ff, d), _BF16),
            pltpu.VMEM((W_SLOTS, d // W_CHUNKS, 2 * d_ff), _F32),
            pltpu.VMEM((W_SLOTS, d_ff // W_CHUNKS, d), _F32),
            pltpu.SemaphoreType.DMA((W_SLOTS,)),
            pltpu.SemaphoreType.DMA((W_SLOTS,)),
        ],
        compiler_params=pltpu.CompilerParams(
            dimension_semantics=("arbitrary",), vmem_limit_bytes=VMEM_LIMIT),
        name=f"ffn_mod{mod_row}",
    )(x2, ada3, g_pre.reshape(1, d), g_post.reshape(1, d), w_in, w_out)


def _mixer_kernel(x_ref, ada_ref, gpre_ref, gpost_ref, wmi_ref, lng_ref, lnb_ref, wsp_ref, bsp_ref,
                  cw_ref, cb_ref, cng_ref, cnb_ref, goa_ref, gob_ref, wmo_ref, o_ref, gbuf,
                  *, tiles_per_seq):
    tm, d = x_ref.shape
    w_a = lng_ref.shape[1]
    w_b = cw_ref.shape[1]
    n_pairs = w_a // LANES

    @pl.when(pl.program_id(0) % tiles_per_seq == 0)
    def _():
        gbuf[0:CONV_HALO, :] = jnp.zeros((CONV_HALO, w_b), _F32)

    ada = ada_ref[0]
    shift, scale, gate = ada[3:4], ada[4:5], ada[5:6]
    g_in = gpre_ref[...] * (1 + scale)
    g_out = gate * gpost_ref[...]

    t_idx = lax.broadcasted_iota(jnp.int32, (CHUNK, 2 * CHUNK), 0)
    s_idx = lax.broadcasted_iota(jnp.int32, (CHUNK, 2 * CHUNK), 1) % CHUNK
    left = lax.broadcasted_iota(jnp.int32, (CHUNK, LANES), 1) < (LANES // 2)
    w_pairs = [jnp.where(s_idx <= t_idx, wsp_ref[p], 0.0).astype(_BF16) for p in range(n_pairs)]
    first_tap = CONV_HALO - (CONV_K - 1)

    def project(t0):
        x = x_ref[t0:t0 + MIX_SUB, :]
        h = (_rms(x, g_in) + shift).astype(_BF16)
        tile = lambda c0: _dot(h, wmi_ref[:, c0:c0 + MXU_N])
        a_cols, g_cols = 2 * w_a, 2 * w_a + w_b
        ag = [(tile(a_cols + j * MXU_N), tile(g_cols + j * MXU_N)) for j in range(w_b // MXU_N)]
        v = _dot(h, wmi_ref[:, w_a:2 * w_a])
        u = _dot(h, wmi_ref[:, 0:w_a])
        a = jnp.concatenate([t[0] for t in ag], axis=1)
        g = jnp.concatenate([t[1] for t in ag], axis=1)
        return x, (u, v, a, g)

    n_sub = tm // MIX_SUB
    nxt = project(0)
    for sub in range(n_sub):
        t0 = sub * MIX_SUB
        x, proj = nxt
        if sub + 1 < n_sub:
            nxt = project(t0 + MIX_SUB)
        u, v, a, g = proj

        gbuf[CONV_HALO + t0:CONV_HALO + t0 + MIX_SUB, :] = a * jax.nn.sigmoid(g)
        conv_rows = []
        for r in range(MIX_SUB // CONV_ROWS):
            r0 = t0 + r * CONV_ROWS
            conv_cols = []
            for c in range(w_b // LANES):
                cols = slice(c * LANES, (c + 1) * LANES)
                acc = None
                for rho in range(SUBLANES):
                    part = None
                    for k in range(CONV_K):
                        off = first_tap + k
                        if off % SUBLANES != rho:
                            continue
                        lo = r0 + off - rho
                        rows = CONV_ROWS + (SUBLANES if rho else 0)
                        term = gbuf[lo:lo + rows, cols] * cw_ref[k:k + 1, cols]
                        part = term if part is None else part + term
                    part = part[rho:rho + CONV_ROWS]
                    acc = part if acc is None else acc + part
                conv_cols.append(acc)
            conv_rows.append(jnp.concatenate(conv_cols, axis=1))
        conv = jnp.concatenate(conv_rows, axis=0) + cb_ref[...]
        y_b = _rms(_silu(_layer_norm(conv, cng_ref[...], cnb_ref[...])), gob_ref[...])

        v = _layer_norm(v, lng_ref[...], lnb_ref[...])
        z_rows = []
        for n in range(MIX_SUB // CHUNK):
            z_cols = []
            for p in range(n_pairs):
                vp = v[n * CHUNK:(n + 1) * CHUNK, p * LANES:(p + 1) * LANES]
                rhs = jnp.concatenate([jnp.where(left, vp, 0.0), jnp.where(left, 0.0, vp)], axis=0)
                z_cols.append(_dot(w_pairs[p], rhs.astype(_BF16)))
            z_rows.append(jnp.concatenate(z_cols, axis=1) + bsp_ref[...])
        y_a = _rms(u * jnp.concatenate(z_rows, axis=0), goa_ref[...])

        y = jnp.concatenate([y_a, y_b], axis=1).astype(_BF16)
        o_ref[t0:t0 + MIX_SUB, :] = x + _rms(_dot(y, wmo_ref[...]), g_out)

    gbuf[0:CONV_HALO, :] = gbuf[tm:tm + CONV_HALO, :]


def _mixer(x2, ada3, g_pre, g_post, w_mix_in, lng, lnb, w_sp_pairs, b_sp_full, conv_w, conv_b,
           cng, cnb, goa, gob, w_mix_out, *, seq):
    m, d = x2.shape
    w_a = lng.shape[0]
    w_b = conv_w.shape[1]
    tm = MIX_TM
    tiles_per_seq = seq // tm
    row = lambda v: v.reshape(1, -1)
    return pl.pallas_call(
        functools.partial(_mixer_kernel, tiles_per_seq=tiles_per_seq),
        grid=(m // tm,),
        in_specs=[
            pl.BlockSpec((tm, d), lambda i: (i, 0)),
            pl.BlockSpec((1, N_MOD, d), lambda i: (i // tiles_per_seq, 0, 0)),
            _resident((1, d)),
            _resident((1, d)),
            _resident(w_mix_in.shape),
            _resident((1, w_a)),
            _resident((1, w_a)),
            _resident(w_sp_pairs.shape),
            _resident(b_sp_full.shape),
            _resident(conv_w.shape),
            _resident((1, w_b)),
            _resident((1, w_b)),
            _resident((1, w_b)),
            _resident((1, w_a)),
            _resident((1, w_b)),
            _resident(w_mix_out.shape),
        ],
        out_specs=pl.BlockSpec((tm, d), lambda i: (i, 0)),
        out_shape=jax.ShapeDtypeStruct((m, d), _F32),
        scratch_shapes=[pltpu.VMEM((CONV_HALO + tm, w_b), _F32)],
        compiler_params=pltpu.CompilerParams(
            dimension_semantics=("arbitrary",), vmem_limit_bytes=VMEM_LIMIT),
        name="mixer",
    )(x2, ada3, row(g_pre), row(g_post), w_mix_in, row(lng), row(lnb), w_sp_pairs, b_sp_full,
      conv_w, row(conv_b), row(cng), row(cnb), row(goa), row(gob), w_mix_out)


def kernel(x, c, w_ada, b_ada, g_pre_f1, g_post_f1, w_f1_in, w_f1_out, g_pre_m, g_post_m, w_mix_in,
           gmlp_norm_g, gmlp_norm_b, w_spatial, b_spatial, conv_w, conv_b, conv_norm_g, conv_norm_b,
           g_out_a, g_out_b, w_mix_out, g_pre_f2, g_post_f2, w_f2_in, w_f2_out):
    bsz, seq, d = x.shape
    depth = w_ada.shape[0]
    w_a = gmlp_norm_g.shape[1]
    hd_a = w_a // H_A
    assert seq % FFN_TM == 0 and FFN_TM % FFN_SUB == 0
    assert d % (W_CHUNKS * SUBLANES) == 0 and w_f1_out.shape[1] % (W_CHUNKS * SUBLANES) == 0
    assert seq % MIX_TM == 0 and MIX_TM % MIX_SUB == 0 and MIX_SUB % CHUNK == 0
    assert 2 * hd_a == LANES and w_spatial.shape[2:] == (CHUNK, CHUNK)
    assert CONV_HALO >= CONV_K - 1 and MIX_SUB % CONV_ROWS == 0

    x2 = x.reshape(bsz * seq, d)
    for l in range(depth):
        ada3 = _ada(c, w_ada[l], b_ada[l]).reshape(bsz, N_MOD, d)
        x2 = _ffn(x2, ada3, g_pre_f1[l], g_post_f1[l], w_f1_in[l], w_f1_out[l], mod_row=0, seq=seq)
        w_sp_pairs = w_spatial[l].reshape(H_A // 2, 2, CHUNK, CHUNK).transpose(0, 2, 1, 3)
        w_sp_pairs = w_sp_pairs.reshape(H_A // 2, CHUNK, 2 * CHUNK)
        b_sp_full = jnp.repeat(b_spatial[l].T, hd_a, axis=1)
        x2 = _mixer(x2, ada3, g_pre_m[l], g_post_m[l], w_mix_in[l].astype(_BF16), gmlp_norm_g[l],
                    gmlp_norm_b[l], w_sp_pairs, b_sp_full, conv_w[l], conv_b[l], conv_norm_g[l],
                    conv_norm_b[l], g_out_a[l], g_out_b[l], w_mix_out[l].astype(_BF16), seq=seq)
        x2 = _ffn(x2, ada3, g_pre_f2[l], g_post_f2[l], w_f2_in[l], w_f2_out[l], mod_row=6, seq=seq)
    return x2.reshape(bsz, seq, d)
```

```python
import functools

import jax
import jax.numpy as jnp
from jax import lax
from jax.experimental import pallas as pl
from jax.experimental.pallas import tpu as pltpu

N_MOD = 9
H_A = 8
CHUNK = 128
CONV_K = 31
EPS = 1e-6
HALF = 0.5

CONV_HALO = 32
CONV_ROWS = 64
LANES = 128
SUBLANES = 8
MXU_N = 256

FFN_TM = 1024
FFN_SUB = 256
MIX_TM = 512
MIX_SUB = 256
ADA_TN = 1536
W_CHUNKS = 32
W_SLOTS = 8
VMEM_LIMIT = 56 * 1024 * 1024

_F32 = jnp.float32
_BF16 = jnp.bfloat16


def _rms(x, g):
    return x * lax.rsqrt(jnp.mean(x * x, axis=-1, keepdims=True) + EPS) * g


def _layer_norm(x, g, b):
    mu = jnp.mean(x, axis=-1, keepdims=True)
    xc = x - mu
    var = jnp.mean(xc * xc, axis=-1, keepdims=True)
    return xc * lax.rsqrt(var + EPS) * g + b


def _silu(x):
    return x * jax.nn.sigmoid(x)


def _dot(a, b):
    return jnp.dot(a, b, preferred_element_type=_F32)


def _ada_kernel(c_ref, wlo_ref, whi_ref, blo_ref, bhi_ref, olo_ref, ohi_ref):
    s = _silu(c_ref[...]).astype(_BF16)
    olo_ref[...] = _dot(s, wlo_ref[...].astype(_BF16)) + blo_ref[...]
    ohi_ref[...] = _dot(s, whi_ref[...].astype(_BF16)) + bhi_ref[...]


def _ada(c, w, b):
    bsz, d = c.shape
    n = w.shape[1]
    nb = n // (2 * ADA_TN)
    b = b.reshape(1, n)
    lo = lambda j: (0, j)
    hi = lambda j: (0, j + nb)
    out_lo, out_hi = pl.pallas_call(
        _ada_kernel,
        grid=(nb,),
        in_specs=[
            pl.BlockSpec((bsz, d), lambda j: (0, 0)),
            pl.BlockSpec((d, ADA_TN), lo),
            pl.BlockSpec((d, ADA_TN), hi),
            pl.BlockSpec((1, ADA_TN), lo),
            pl.BlockSpec((1, ADA_TN), hi),
        ],
        out_specs=[pl.BlockSpec((bsz, ADA_TN), lo), pl.BlockSpec((bsz, ADA_TN), lo)],
        out_shape=[jax.ShapeDtypeStruct((bsz, n // 2), _F32)] * 2,
        compiler_params=pltpu.CompilerParams(
            dimension_semantics=("parallel",), vmem_limit_bytes=VMEM_LIMIT),
        name="ada_proj",
    )(c, w, w, b, b)
    return jnp.concatenate([out_lo, out_hi], axis=1)


def _stage_bf16(src_hbm, dst_ref, stage_ref, sems):
    n_slots, rows, _ = stage_ref.shape
    n_chunks = src_hbm.shape[0] // rows

    def copy(i):
        slot = i % n_slots
        return pltpu.make_async_copy(src_hbm.at[pl.ds(i * rows, rows), :], stage_ref.at[slot], sems.at[slot])

    for i in range(min(n_slots - 1, n_chunks)):
        copy(i).start()
    for i in range(n_chunks):
        if i + n_slots - 1 < n_chunks:
            copy(i + n_slots - 1).start()
        copy(i).wait()
        dst_ref[i * rows:(i + 1) * rows, :] = stage_ref[i % n_slots].astype(_BF16)


def _ffn_kernel(x_ref, ada_ref, gpre_ref, gpost_ref, win_hbm, wout_hbm, o_ref,
                win_ref, wout_ref, stage_in, stage_out, sem_in, sem_out, *, mod_row, d_ff):
    @pl.when(pl.program_id(0) == 0)
    def _():
        _stage_bf16(win_hbm, win_ref, stage_in, sem_in)
        _stage_bf16(wout_hbm, wout_ref, stage_out, sem_out)

    ada = ada_ref[0]
    shift = ada[mod_row:mod_row + 1]
    scale = ada[mod_row + 1:mod_row + 2]
    gate = ada[mod_row + 2:mod_row + 3]
    g_in = gpre_ref[...] * (1 + scale)
    g_out = (HALF * gate) * gpost_ref[...]
    for s in range(x_ref.shape[0] // FFN_SUB):
        rows = slice(s * FFN_SUB, (s + 1) * FFN_SUB)
        x = x_ref[rows, :]
        h = (_rms(x, g_in) + shift).astype(_BF16)
        act = []
        for j in range(d_ff // MXU_N):
            g = _dot(h, win_ref[:, j * MXU_N:(j + 1) * MXU_N])
            u = _dot(h, win_ref[:, d_ff + j * MXU_N:d_ff + (j + 1) * MXU_N])
            act.append(_silu(g) * u)
        act = jnp.concatenate(act, axis=1).astype(_BF16)
        y = _dot(act, wout_ref[...])
        o_ref[rows, :] = x + _rms(y, g_out)


def _resident(shape):
    return pl.BlockSpec(shape, lambda i: (0,) * len(shape), pipeline_mode=pl.Buffered(1))


def _ffn(x2, ada3, g_pre, g_post, w_in, w_out, *, mod_row, seq):
    m, d = x2.shape
    d_ff = w_out.shape[0]
    tm = FFN_TM
    tiles_per_seq = seq // tm
    return pl.pallas_call(
        functools.partial(_ffn_kernel, mod_row=mod_row, d_ff=d_ff),
        grid=(m // tm,),
        in_specs=[
            pl.BlockSpec((tm, d), lambda i: (i, 0)),
            pl.BlockSpec((1, N_MOD, d), lambda i: (i // tiles_per_seq, 0, 0)),
            _resident((1, d)),
            _resident((1, d)),
            pl.BlockSpec(memory_space=pl.ANY),
            pl.BlockSpec(memory_space=pl.ANY),
        ],
        out_specs=pl.BlockSpec((tm, d), lambda i: (i, 0)),
        out_shape=jax.ShapeDtypeStruct((m, d), _F32),
        scratch_shapes=[
            pltpu.VMEM((d, 2 * d_ff), _BF16),
            pltpu.VMEM((d_ff, d), _BF16),
            pltpu.VMEM((W_SLOTS, d // W_CHUNKS, 2 * d_ff), _F32),
            pltpu.VMEM((W_SLOTS, d_ff // W_CHUNKS, d), _F32),
            pltpu.SemaphoreType.DMA((W_SLOTS,)),
            pltpu.SemaphoreType.DMA((W_SLOTS,)),
        ],
        compiler_params=pltpu.CompilerParams(
            dimension_semantics=("arbitrary",), vmem_limit_bytes=VMEM_LIMIT),
        name=f"ffn_mod{mod_row}",
    )(x2, ada3, g_pre.reshape(1, d), g_post.reshape(1, d), w_in, w_out)


def _mixer_kernel(x_ref, ada_ref, gpre_ref, gpost_ref, wmi_ref, lng_ref, lnb_ref, wsp_ref, bsp_ref,
                  cw_ref, cb_ref, cng_ref, cnb_ref, goa_ref, gob_ref, wmo_ref, o_ref, gbuf,
                  *, tiles_per_seq):
    tm, d = x_ref.shape
    w_a = lng_ref.shape[1]
    w_b = cw_ref.shape[1]
    n_pairs = w_a // LANES

    @pl.when(pl.program_id(0) % tiles_per_seq == 0)
    def _():
        gbuf[0:CONV_HALO, :] = jnp.zeros((CONV_HALO, w_b), _F32)

    ada = ada_ref[0]
    shift, scale, gate = ada[3:4], ada[4:5], ada[5:6]
    g_in = gpre_ref[...] * (1 + scale)
    g_out = gate * gpost_ref[...]

    t_idx = lax.broadcasted_iota(jnp.int32, (CHUNK, 2 * CHUNK), 0)
    s_idx = lax.broadcasted_iota(jnp.int32, (CHUNK, 2 * CHUNK), 1) % CHUNK
    left = lax.broadcasted_iota(jnp.int32, (CHUNK, LANES), 1) < (LANES // 2)
    w_pairs = [jnp.where(s_idx <= t_idx, wsp_ref[p], 0.0).astype(_BF16) for p in range(n_pairs)]
    first_tap = CONV_HALO - (CONV_K - 1)

    def project(t0):
        x = x_ref[t0:t0 + MIX_SUB, :]
        h = (_rms(x, g_in) + shift).astype(_BF16)
        tile = lambda c0: _dot(h, wmi_ref[:, c0:c0 + MXU_N])
        a_cols, g_cols = 2 * w_a, 2 * w_a + w_b
        ag = [(tile(a_cols + j * MXU_N), tile(g_cols + j * MXU_N)) for j in range(w_b // MXU_N)]
        v = _dot(h, wmi_ref[:, w_a:2 * w_a])
        u = _dot(h, wmi_ref[:, 0:w_a])
        a = jnp.concatenate([t[0] for t in ag], axis=1)
        g = jnp.concatenate([t[1] for t in ag], axis=1)
        return x, (u, v, a, g)

    n_sub = tm // MIX_SUB
    nxt = project(0)
    for sub in range(n_sub):
        t0 = sub * MIX_SUB
        x, proj = nxt
        if sub + 1 < n_sub:
            nxt = project(t0 + MIX_SUB)
        u, v, a, g = proj

        gbuf[CONV_HALO + t0:CONV_HALO + t0 + MIX_SUB, :] = a * jax.nn.sigmoid(g)
        conv_rows = []
        for r in range(MIX_SUB // CONV_ROWS):
            r0 = t0 + r * CONV_ROWS
            conv_cols = []
            for c in range(w_b // LANES):
                cols = slice(c * LANES, (c + 1) * LANES)
                acc = None
                for rho in range(SUBLANES):
                    part = None
                    for k in range(CONV_K):
                        off = first_tap + k
                        if off % SUBLANES != rho:
                            continue
                        lo = r0 + off - rho
                        rows = CONV_ROWS + (SUBLANES if rho else 0)
                        term = gbuf[lo:lo + rows, cols] * cw_ref[k:k + 1, cols]
                        part = term if part is None else part + term
                    part = part[rho:rho + CONV_ROWS]
                    acc = part if acc is None else acc + part
                conv_cols.append(acc)
            conv_rows.append(jnp.concatenate(conv_cols, axis=1))
        conv = jnp.concatenate(conv_rows, axis=0) + cb_ref[...]
        y_b = _rms(_silu(_layer_norm(conv, cng_ref[...], cnb_ref[...])), gob_ref[...])

        v = _layer_norm(v, lng_ref[...], lnb_ref[...])
        z_rows = []
        for n in range(MIX_SUB // CHUNK):
            z_cols = []
            for p in range(n_pairs):
                vp = v[n * CHUNK:(n + 1) * CHUNK, p * LANES:(p + 1) * LANES]
                rhs = jnp.concatenate([jnp.where(left, vp, 0.0), jnp.where(left, 0.0, vp)], axis=0)
                z_cols.append(_dot(w_pairs[p], rhs.astype(_BF16)))
            z_rows.append(jnp.concatenate(z_cols, axis=1) + bsp_ref[...])
        y_a = _rms(u * jnp.concatenate(z_rows, axis=0), goa_ref[...])

        y = jnp.concatenate([y_a, y_b], axis=1).astype(_BF16)
        o_ref[t0:t0 + MIX_SUB, :] = x + _rms(_dot(y, wmo_ref[...]), g_out)

    gbuf[0:CONV_HALO, :] = gbuf[tm:tm + CONV_HALO, :]


def _mixer(x2, ada3, g_pre, g_post, w_mix_in, lng, lnb, w_sp_pairs, b_sp_full, conv_w, conv_b,
           cng, cnb, goa, gob, w_mix_out, *, seq):
    m, d = x2.shape
    w_a = lng.shape[0]
    w_b = conv_w.shape[1]
    tm = MIX_TM
    tiles_per_seq = seq // tm
    row = lambda v: v.reshape(1, -1)
    return pl.pallas_call(
        functools.partial(_mixer_kernel, tiles_per_seq=tiles_per_seq),
        grid=(m // tm,),
        in_specs=[
            pl.BlockSpec((tm, d), lambda i: (i, 0)),
            pl.BlockSpec((1, N_MOD, d), lambda i: (i // tiles_per_seq, 0, 0)),
            _resident((1, d)),
            _resident((1, d)),
            _resident(w_mix_in.shape),
            _resident((1, w_a)),
            _resident((1, w_a)),
            _resident(w_sp_pairs.shape),
            _resident(b_sp_full.shape),
            _resident(conv_w.shape),
            _resident((1, w_b)),
            _resident((1, w_b)),
            _resident((1, w_b)),
            _resident((1, w_a)),
            _resident((1, w_b)),
            _resident(w_mix_out.shape),
        ],
        out_specs=pl.BlockSpec((tm, d), lambda i: (i, 0)),
        out_shape=jax.ShapeDtypeStruct((m, d), _F32),
        scratch_shapes=[pltpu.VMEM((CONV_HALO + tm, w_b), _F32)],
        compiler_params=pltpu.CompilerParams(
            dimension_semantics=("arbitrary",), vmem_limit_bytes=VMEM_LIMIT),
        name="mixer",
    )(x2, ada3, row(g_pre), row(g_post), w_mix_in, row(lng), row(lnb), w_sp_pairs, b_sp_full,
      conv_w, row(conv_b), row(cng), row(cnb), row(goa), row(gob), w_mix_out)


def kernel(x, c, w_ada, b_ada, g_pre_f1, g_post_f1, w_f1_in, w_f1_out, g_pre_m, g_post_m, w_mix_in,
           gmlp_norm_g, gmlp_norm_b, w_spatial, b_spatial, conv_w, conv_b, conv_norm_g, conv_norm_b,
           g_out_a, g_out_b, w_mix_out, g_pre_f2, g_post_f2, w_f2_in, w_f2_out):
    bsz, seq, d = x.shape
    depth = w_ada.shape[0]
    w_a = gmlp_norm_g.shape[1]
    hd_a = w_a // H_A
    assert seq % FFN_TM == 0 and FFN_TM % FFN_SUB == 0 and (N_MOD * d) % (2 * ADA_TN) == 0
    assert d % (W_CHUNKS * SUBLANES) == 0 and w_f1_out.shape[1] % (W_CHUNKS * SUBLANES) == 0
    assert seq % MIX_TM == 0 and MIX_TM % MIX_SUB == 0 and MIX_SUB % CHUNK == 0
    assert 2 * hd_a == LANES and w_spatial.shape[2:] == (CHUNK, CHUNK)
    assert CONV_HALO >= CONV_K - 1 and MIX_SUB % CONV_ROWS == 0

    x2 = x.reshape(bsz * seq, d)
    for l in range(depth):
        ada3 = _ada(c, w_ada[l], b_ada[l]).reshape(bsz, N_MOD, d)
        x2 = _ffn(x2, ada3, g_pre_f1[l], g_post_f1[l], w_f1_in[l], w_f1_out[l], mod_row=0, seq=seq)
        w_sp_pairs = w_spatial[l].reshape(H_A // 2, 2, CHUNK, CHUNK).transpose(0, 2, 1, 3)
        w_sp_pairs = w_sp_pairs.reshape(H_A // 2, CHUNK, 2 * CHUNK)
        b_sp_full = jnp.repeat(b_spatial[l].T, hd_a, axis=1)
        x2 = _mixer(x2, ada3, g_pre_m[l], g_post_m[l], w_mix_in[l].astype(_BF16), gmlp_norm_g[l],
                    gmlp_norm_b[l], w_sp_pairs, b_sp_full, conv_w[l], conv_b[l], conv_norm_g[l],
                    conv_norm_b[l], g_out_a[l], g_out_b[l], w_mix_out[l].astype(_BF16), seq=seq)
        x2 = _ffn(x2, ada3, g_pre_f2[l], g_post_f2[l], w_f2_in[l], w_f2_out[l], mod_row=6, seq=seq)
    return x2.reshape(bsz, seq, d)
```

```python
import functools

import jax
import jax.numpy as jnp
from jax import lax
from jax.experimental import pallas as pl
from jax.experimental.pallas import tpu as pltpu

N_MOD = 9
H_A = 8
CHUNK = 128
CONV_K = 31
EPS = 1e-6
HALF = 0.5

CONV_HALO = 32
CONV_ROWS = 64
LANES = 128
SUBLANES = 8
MXU_N = 256
BF16_ROWS = 16
ANCHOR_FROM = 3

FFN_TM = 1024
FFN_SUB = 256
MIX_TM = 512
MIX_SUB = 256
ADA_TN = 1536
W_CHUNKS = 32
W_SLOTS = 8
VMEM_LIMIT = 56 * 1024 * 1024

_F32 = jnp.float32
_BF16 = jnp.bfloat16


def _rms(x, g):
    return x * lax.rsqrt(jnp.mean(x * x, axis=-1, keepdims=True) + EPS) * g


def _layer_norm(x, g, b):
    mu = jnp.mean(x, axis=-1, keepdims=True)
    xc = x - mu
    var = jnp.mean(xc * xc, axis=-1, keepdims=True)
    return xc * lax.rsqrt(var + EPS) * g + b


def _silu(x):
    return x * jax.nn.sigmoid(x)


def _dot(a, b):
    return jnp.dot(a, b, preferred_element_type=_F32)


def _anchored(h, anchor):
    bits = lax.bitcast_convert_type(anchor.astype(_F32), jnp.uint32)
    zero = lax.shift_right_logical(lax.shift_right_logical(bits, jnp.uint32(16)), jnp.uint32(16))
    zero = lax.bitcast_convert_type(zero, _F32).astype(h.dtype)
    r, c = anchor.shape
    top = jnp.concatenate([h[0:r, 0:c] + zero, h[0:r, c:]], axis=1)
    return jnp.concatenate([top, h[r:]], axis=0)


def _ada_kernel(c_ref, wlo_ref, whi_ref, blo_ref, bhi_ref, olo_ref, ohi_ref):
    s = _silu(c_ref[...]).astype(_BF16)
    olo_ref[...] = _dot(s, wlo_ref[...].astype(_BF16)) + blo_ref[...]
    ohi_ref[...] = _dot(s, whi_ref[...].astype(_BF16)) + bhi_ref[...]


def _ada(c, w, b):
    bsz, d = c.shape
    n = w.shape[1]
    nb = n // (2 * ADA_TN)
    b = b.reshape(1, n)
    lo = lambda j: (0, j)
    hi = lambda j: (0, j + nb)
    out_lo, out_hi = pl.pallas_call(
        _ada_kernel,
        grid=(nb,),
        in_specs=[
            pl.BlockSpec((bsz, d), lambda j: (0, 0)),
            pl.BlockSpec((d, ADA_TN), lo),
            pl.BlockSpec((d, ADA_TN), hi),
            pl.BlockSpec((1, ADA_TN), lo),
            pl.BlockSpec((1, ADA_TN), hi),
        ],
        out_specs=[pl.BlockSpec((bsz, ADA_TN), lo), pl.BlockSpec((bsz, ADA_TN), lo)],
        out_shape=[jax.ShapeDtypeStruct((bsz, n // 2), _F32)] * 2,
        compiler_params=pltpu.CompilerParams(
            dimension_semantics=("parallel",), vmem_limit_bytes=VMEM_LIMIT),
        name="ada_proj",
    )(c, w, w, b, b)
    return jnp.concatenate([out_lo, out_hi], axis=1)


def _stage_bf16(src_hbm, dst_ref, stage_ref, sems):
    n_slots, rows, _ = stage_ref.shape
    n_chunks = src_hbm.shape[0] // rows

    def copy(i):
        slot = i % n_slots
        return pltpu.make_async_copy(src_hbm.at[pl.ds(i * rows, rows), :], stage_ref.at[slot], sems.at[slot])

    for i in range(min(n_slots - 1, n_chunks)):
        copy(i).start()
    for i in range(n_chunks):
        if i + n_slots - 1 < n_chunks:
            copy(i + n_slots - 1).start()
        copy(i).wait()
        dst_ref[i * rows:(i + 1) * rows, :] = stage_ref[i % n_slots].astype(_BF16)


def _ffn_kernel(x_ref, ada_ref, gpre_ref, gpost_ref, win_hbm, wout_hbm, o_ref,
                win_ref, wout_ref, stage_in, stage_out, sem_in, sem_out, *, mod_row, d_ff):
    @pl.when(pl.program_id(0) == 0)
    def _():
        _stage_bf16(win_hbm, win_ref, stage_in, sem_in)
        _stage_bf16(wout_hbm, wout_ref, stage_out, sem_out)

    ada = ada_ref[0]
    shift = ada[mod_row:mod_row + 1]
    scale = ada[mod_row + 1:mod_row + 2]
    gate = ada[mod_row + 2:mod_row + 3]
    g_in = gpre_ref[...] * (1 + scale)
    g_out = (HALF * gate) * gpost_ref[...]
    n_sub = x_ref.shape[0] // FFN_SUB
    n_tiles = d_ff // MXU_N
    n_groups = FFN_SUB // BF16_ROWS
    rows = [slice(s * FFN_SUB, (s + 1) * FFN_SUB) for s in range(n_sub)]
    hs = [(_rms(x_ref[r, :], g_in) + shift).astype(_BF16) for r in rows]
    for s in range(n_sub):
        x = x_ref[rows[s], :]
        act = []
        for j in range(n_tiles):
            h = hs[s]
            if s + 1 < n_sub and j >= ANCHOR_FROM:
                for k in range(j - ANCHOR_FROM, n_groups, n_tiles - ANCHOR_FROM):
                    h = _anchored(h, hs[s + 1][k * BF16_ROWS:(k + 1) * BF16_ROWS, 0:LANES])
            g = _dot(h, win_ref[:, j * MXU_N:(j + 1) * MXU_N])
            u = _dot(h, win_ref[:, d_ff + j * MXU_N:d_ff + (j + 1) * MXU_N])
            act.append(_silu(g) * u)
        act = jnp.concatenate(act, axis=1).astype(_BF16)
        y = _dot(act, wout_ref[...])
        o_ref[rows[s], :] = x + _rms(y, g_out)


def _resident(shape):
    return pl.BlockSpec(shape, lambda i: (0,) * len(shape), pipeline_mode=pl.Buffered(1))


def _ffn(x2, ada3, g_pre, g_post, w_in, w_out, *, mod_row, seq):
    m, d = x2.shape
    d_ff = w_out.shape[0]
    tm = FFN_TM
    tiles_per_seq = seq // tm
    return pl.pallas_call(
        functools.partial(_ffn_kernel, mod_row=mod_row, d_ff=d_ff),
        grid=(m // tm,),
        in_specs=[
            pl.BlockSpec((tm, d), lambda i: (i, 0)),
            pl.BlockSpec((1, N_MOD, d), lambda i: (i // tiles_per_seq, 0, 0)),
            _resident((1, d)),
            _resident((1, d)),
            pl.BlockSpec(memory_space=pl.ANY),
            pl.BlockSpec(memory_space=pl.ANY),
        ],
        out_specs=pl.BlockSpec((tm, d), lambda i: (i, 0)),
        out_shape=jax.ShapeDtypeStruct((m, d), _F32),
        scratch_shapes=[
            pltpu.VMEM((d, 2 * d_ff), _BF16),
            pltpu.VMEM((d_ff, d), _BF16),
            pltpu.VMEM((W_SLOTS, d // W_CHUNKS, 2 * d_ff), _F32),
            pltpu.VMEM((W_SLOTS, d_ff // W_CHUNKS, d), _F32),
            pltpu.SemaphoreType.DMA((W_SLOTS,)),
            pltpu.SemaphoreType.DMA((W_SLOTS,)),
        ],
        compiler_params=pltpu.CompilerParams(
            dimension_semantics=("arbitrary",), vmem_limit_bytes=VMEM_LIMIT),
        name=f"ffn_mod{mod_row}",
    )(x2, ada3, g_pre.reshape(1, d), g_post.reshape(1, d), w_in, w_out)


def _mixer_kernel(x_ref, ada_ref, gpre_ref, gpost_ref, wmi_ref, lng_ref, lnb_ref, wsp_ref, bsp_ref,
                  cw_ref, cb_ref, cng_ref, cnb_ref, goa_ref, gob_ref, wmo_ref, o_ref, gbuf,
                  *, tiles_per_seq):
    tm, d = x_ref.shape
    w_a = lng_ref.shape[1]
    w_b = cw_ref.shape[1]
    n_pairs = w_a // LANES

    @pl.when(pl.program_id(0) % tiles_per_seq == 0)
    def _():
        gbuf[0:CONV_HALO, :] = jnp.zeros((CONV_HALO, w_b), _F32)

    ada = ada_ref[0]
    shift, scale, gate = ada[3:4], ada[4:5], ada[5:6]
    g_in = gpre_ref[...] * (1 + scale)
    g_out = gate * gpost_ref[...]

    t_idx = lax.broadcasted_iota(jnp.int32, (CHUNK, 2 * CHUNK), 0)
    s_idx = lax.broadcasted_iota(jnp.int32, (CHUNK, 2 * CHUNK), 1) % CHUNK
    left = lax.broadcasted_iota(jnp.int32, (CHUNK, LANES), 1) < (LANES // 2)
    w_pairs = [jnp.where(s_idx <= t_idx, wsp_ref[p], 0.0).astype(_BF16) for p in range(n_pairs)]
    first_tap = CONV_HALO - (CONV_K - 1)

    def project(t0):
        x = x_ref[t0:t0 + MIX_SUB, :]
        h = (_rms(x, g_in) + shift).astype(_BF16)
        tile = lambda c0: _dot(h, wmi_ref[:, c0:c0 + MXU_N])
        a_cols, g_cols = 2 * w_a, 2 * w_a + w_b
        ag = [(tile(a_cols + j * MXU_N), tile(g_cols + j * MXU_N)) for j in range(w_b // MXU_N)]
        v = _dot(h, wmi_ref[:, w_a:2 * w_a])
        u = _dot(h, wmi_ref[:, 0:w_a])
        a = jnp.concatenate([t[0] for t in ag], axis=1)
        g = jnp.concatenate([t[1] for t in ag], axis=1)
        return x, (u, v, a, g)

    n_sub = tm // MIX_SUB
    nxt = project(0)
    for sub in range(n_sub):
        t0 = sub * MIX_SUB
        x, proj = nxt
        if sub + 1 < n_sub:
            nxt = project(t0 + MIX_SUB)
        u, v, a, g = proj

        gbuf[CONV_HALO + t0:CONV_HALO + t0 + MIX_SUB, :] = a * jax.nn.sigmoid(g)
        conv_rows = []
        for r in range(MIX_SUB // CONV_ROWS):
            r0 = t0 + r * CONV_ROWS
            conv_cols = []
            for c in range(w_b // LANES):
                cols = slice(c * LANES, (c + 1) * LANES)
                acc = None
                for rho in range(SUBLANES):
                    part = None
                    for k in range(CONV_K):
                        off = first_tap + k
                        if off % SUBLANES != rho:
                            continue
                        lo = r0 + off - rho
                        rows = CONV_ROWS + (SUBLANES if rho else 0)
                        term = gbuf[lo:lo + rows, cols] * cw_ref[k:k + 1, cols]
                        part = term if part is None else part + term
                    part = part[rho:rho + CONV_ROWS]
                    acc = part if acc is None else acc + part
                conv_cols.append(acc)
            conv_rows.append(jnp.concatenate(conv_cols, axis=1))
        conv = jnp.concatenate(conv_rows, axis=0) + cb_ref[...]
        y_b = _rms(_silu(_layer_norm(conv, cng_ref[...], cnb_ref[...])), gob_ref[...])

        v = _layer_norm(v, lng_ref[...], lnb_ref[...])
        z_rows = []
        for n in range(MIX_SUB // CHUNK):
            z_cols = []
            for p in range(n_pairs):
                vp = v[n * CHUNK:(n + 1) * CHUNK, p * LANES:(p + 1) * LANES]
                rhs = jnp.concatenate([jnp.where(left, vp, 0.0), jnp.where(left, 0.0, vp)], axis=0)
                z_cols.append(_dot(w_pairs[p], rhs.astype(_BF16)))
            z_rows.append(jnp.concatenate(z_cols, axis=1) + bsp_ref[...])
        y_a = _rms(u * jnp.concatenate(z_rows, axis=0), goa_ref[...])

        y = jnp.concatenate([y_a, y_b], axis=1).astype(_BF16)
        o_ref[t0:t0 + MIX_SUB, :] = x + _rms(_dot(y, wmo_ref[...]), g_out)

    gbuf[0:CONV_HALO, :] = gbuf[tm:tm + CONV_HALO, :]


def _mixer(x2, ada3, g_pre, g_post, w_mix_in, lng, lnb, w_sp_pairs, b_sp_full, conv_w, conv_b,
           cng, cnb, goa, gob, w_mix_out, *, seq):
    m, d = x2.shape
    w_a = lng.shape[0]
    w_b = conv_w.shape[1]
    tm = MIX_TM
    tiles_per_seq = seq // tm
    row = lambda v: v.reshape(1, -1)
    return pl.pallas_call(
        functools.partial(_mixer_kernel, tiles_per_seq=tiles_per_seq),
        grid=(m // tm,),
        in_specs=[
            pl.BlockSpec((tm, d), lambda i: (i, 0)),
            pl.BlockSpec((1, N_MOD, d), lambda i: (i // tiles_per_seq, 0, 0)),
            _resident((1, d)),
            _resident((1, d)),
            _resident(w_mix_in.shape),
            _resident((1, w_a)),
            _resident((1, w_a)),
            _resident(w_sp_pairs.shape),
            _resident(b_sp_full.shape),
            _resident(conv_w.shape),
            _resident((1, w_b)),
            _resident((1, w_b)),
            _resident((1, w_b)),
            _resident((1, w_a)),
            _resident((1, w_b)),
            _resident(w_mix_out.shape),
        ],
        out_specs=pl.BlockSpec((tm, d), lambda i: (i, 0)),
        out_shape=jax.ShapeDtypeStruct((m, d), _F32),
        scratch_shapes=[pltpu.VMEM((CONV_HALO + tm, w_b), _F32)],
        compiler_params=pltpu.CompilerParams(
            dimension_semantics=("arbitrary",), vmem_limit_bytes=VMEM_LIMIT),
        name="mixer",
    )(x2, ada3, row(g_pre), row(g_post), w_mix_in, row(lng), row(lnb), w_sp_pairs, b_sp_full,
      conv_w, row(conv_b), row(cng), row(cnb), row(goa), row(gob), w_mix_out)


def kernel(x, c, w_ada, b_ada, g_pre_f1, g_post_f1, w_f1_in, w_f1_out, g_pre_m, g_post_m, w_mix_in,
           gmlp_norm_g, gmlp_norm_b, w_spatial, b_spatial, conv_w, conv_b, conv_norm_g, conv_norm_b,
           g_out_a, g_out_b, w_mix_out, g_pre_f2, g_post_f2, w_f2_in, w_f2_out):
    bsz, seq, d = x.shape
    depth = w_ada.shape[0]
    w_a = gmlp_norm_g.shape[1]
    hd_a = w_a // H_A
    assert seq % FFN_TM == 0 and FFN_TM % FFN_SUB == 0 and (N_MOD * d) % (2 * ADA_TN) == 0
    assert d % (W_CHUNKS * SUBLANES) == 0 and w_f1_out.shape[1] % (W_CHUNKS * SUBLANES) == 0
    assert seq % MIX_TM == 0 and MIX_TM % MIX_SUB == 0 and MIX_SUB % CHUNK == 0
    assert 2 * hd_a == LANES and w_spatial.shape[2:] == (CHUNK, CHUNK)
    assert CONV_HALO >= CONV_K - 1 and MIX_SUB % CONV_ROWS == 0

    x2 = x.reshape(bsz * seq, d)
    for l in range(depth):
        ada3 = _ada(c, w_ada[l], b_ada[l]).reshape(bsz, N_MOD, d)
        x2 = _ffn(x2, ada3, g_pre_f1[l], g_post_f1[l], w_f1_in[l], w_f1_out[l], mod_row=0, seq=seq)
        w_sp_pairs = w_spatial[l].reshape(H_A // 2, 2, CHUNK, CHUNK).transpose(0, 2, 1, 3)
        w_sp_pairs = w_sp_pairs.reshape(H_A // 2, CHUNK, 2 * CHUNK)
        b_sp_full = jnp.repeat(b_spatial[l].T, hd_a, axis=1)
        x2 = _mixer(x2, ada3, g_pre_m[l], g_post_m[l], w_mix_in[l].astype(_BF16), gmlp_norm_g[l],
                    gmlp_norm_b[l], w_sp_pairs, b_sp_full, conv_w[l], conv_b[l], conv_norm_g[l],
                    conv_norm_b[l], g_out_a[l], g_out_b[l], w_mix_out[l].astype(_BF16), seq=seq)
        x2 = _ffn(x2, ada3, g_pre_f2[l], g_post_f2[l], w_f2_in[l], w_f2_out[l], mod_row=6, seq=seq)
    return x2.reshape(bsz, seq, d)
```

```python
import functools

import jax
import jax.numpy as jnp
from jax import lax
from jax.experimental import pallas as pl
from jax.experimental.pallas import tpu as pltpu

N_MOD = 9
H_A = 8
CHUNK = 128
CONV_K = 31
EPS = 1e-6
HALF = 0.5

CONV_HALO = 32
CONV_ROWS = 64
LANES = 128
SUBLANES = 8
MXU_N = 256
BF16_ROWS = 16
ANCHOR_FROM = 3

FFN_TM = 1024
FFN_SUB = 256
MIX_TM = 512
MIX_SUB = 256
ADA_TN = 1536
W_CHUNKS = 32
W_SLOTS = 8
VMEM_LIMIT = 56 * 1024 * 1024

_F32 = jnp.float32
_BF16 = jnp.bfloat16


def _rms(x, g):
    return x * lax.rsqrt(jnp.mean(x * x, axis=-1, keepdims=True) + EPS) * g


def _layer_norm(x, g, b):
    mu = jnp.mean(x, axis=-1, keepdims=True)
    xc = x - mu
    var = jnp.mean(xc * xc, axis=-1, keepdims=True)
    return xc * lax.rsqrt(var + EPS) * g + b


def _silu(x):
    return x * jax.nn.sigmoid(x)


def _dot(a, b):
    return jnp.dot(a, b, preferred_element_type=_F32)


def _anchored(h, anchor):
    bits = lax.bitcast_convert_type(anchor.astype(_F32), jnp.uint32)
    zero = lax.shift_right_logical(lax.shift_right_logical(bits, jnp.uint32(16)), jnp.uint32(16))
    zero = lax.bitcast_convert_type(zero, _F32).astype(h.dtype)
    r, c = anchor.shape
    top = jnp.concatenate([h[0:r, 0:c] + zero, h[0:r, c:]], axis=1)
    return jnp.concatenate([top, h[r:]], axis=0)


def _ada_kernel(c_ref, wlo_ref, whi_ref, blo_ref, bhi_ref, olo_ref, ohi_ref):
    s = _silu(c_ref[...]).astype(_BF16)
    olo_ref[...] = _dot(s, wlo_ref[...].astype(_BF16)) + blo_ref[...]
    ohi_ref[...] = _dot(s, whi_ref[...].astype(_BF16)) + bhi_ref[...]


def _ada(c, w, b):
    bsz, d = c.shape
    n = w.shape[1]
    nb = n // (2 * ADA_TN)
    b = b.reshape(1, n)
    lo = lambda j: (0, j)
    hi = lambda j: (0, j + nb)
    out_lo, out_hi = pl.pallas_call(
        _ada_kernel,
        grid=(nb,),
        in_specs=[
            pl.BlockSpec((bsz, d), lambda j: (0, 0)),
            pl.BlockSpec((d, ADA_TN), lo),
            pl.BlockSpec((d, ADA_TN), hi),
            pl.BlockSpec((1, ADA_TN), lo),
            pl.BlockSpec((1, ADA_TN), hi),
        ],
        out_specs=[pl.BlockSpec((bsz, ADA_TN), lo), pl.BlockSpec((bsz, ADA_TN), lo)],
        out_shape=[jax.ShapeDtypeStruct((bsz, n // 2), _F32)] * 2,
        compiler_params=pltpu.CompilerParams(
            dimension_semantics=("parallel",), vmem_limit_bytes=VMEM_LIMIT),
        name="ada_proj",
    )(c, w, w, b, b)
    return jnp.concatenate([out_lo, out_hi], axis=1)


def _stage_bf16(src_hbm, dst_ref, stage_ref, sems):
    n_slots, rows, _ = stage_ref.shape
    n_chunks = src_hbm.shape[0] // rows

    def copy(i):
        slot = i % n_slots
        return pltpu.make_async_copy(src_hbm.at[pl.ds(i * rows, rows), :], stage_ref.at[slot], sems.at[slot])

    for i in range(min(n_slots - 1, n_chunks)):
        copy(i).start()
    for i in range(n_chunks):
        if i + n_slots - 1 < n_chunks:
            copy(i + n_slots - 1).start()
        copy(i).wait()
        dst_ref[i * rows:(i + 1) * rows, :] = stage_ref[i % n_slots].astype(_BF16)


def _cast_blocks(refs, n_cast, n_before):
    srcs = refs[n_before:n_before + n_cast]
    dsts = refs[n_before + n_cast + 1:n_before + 2 * n_cast + 1]
    for src, dst in zip(srcs, dsts):
        dst[...] = src[...].astype(_BF16)


def _ffn_kernel(*refs, mod_row, d_ff, n_cast, staged):
    x_ref, ada_ref, gpre_ref, gpost_ref, win_ref, wout_ref = refs[:6]
    o_ref = refs[6 + n_cast]
    _cast_blocks(refs, n_cast, 6)
    if staged:
        win_hbm, wout_hbm = win_ref, wout_ref
        win_ref, wout_ref, stage_in, stage_out, sem_in, sem_out = refs[7 + 2 * n_cast:]

        @pl.when(pl.program_id(0) == 0)
        def _():
            _stage_bf16(win_hbm, win_ref, stage_in, sem_in)
            _stage_bf16(wout_hbm, wout_ref, stage_out, sem_out)

    ada = ada_ref[0]
    shift = ada[mod_row:mod_row + 1]
    scale = ada[mod_row + 1:mod_row + 2]
    gate = ada[mod_row + 2:mod_row + 3]
    g_in = gpre_ref[...] * (1 + scale)
    g_out = (HALF * gate) * gpost_ref[...]
    n_sub = x_ref.shape[0] // FFN_SUB
    n_tiles = d_ff // MXU_N
    n_groups = FFN_SUB // BF16_ROWS
    rows = [slice(s * FFN_SUB, (s + 1) * FFN_SUB) for s in range(n_sub)]
    hs = [(_rms(x_ref[r, :], g_in) + shift).astype(_BF16) for r in rows]
    for s in range(n_sub):
        x = x_ref[rows[s], :]
        act = []
        for j in range(n_tiles):
            h = hs[s]
            if s + 1 < n_sub and j >= ANCHOR_FROM:
                for k in range(j - ANCHOR_FROM, n_groups, n_tiles - ANCHOR_FROM):
                    h = _anchored(h, hs[s + 1][k * BF16_ROWS:(k + 1) * BF16_ROWS, 0:LANES])
            g = _dot(h, win_ref[:, j * MXU_N:(j + 1) * MXU_N])
            u = _dot(h, win_ref[:, d_ff + j * MXU_N:d_ff + (j + 1) * MXU_N])
            act.append(_silu(g) * u)
        act = jnp.concatenate(act, axis=1).astype(_BF16)
        y = _dot(act, wout_ref[...])
        o_ref[rows[s], :] = x + _rms(y, g_out)


def _resident(shape):
    return pl.BlockSpec(shape, lambda i: (0,) * len(shape), pipeline_mode=pl.Buffered(1))


def _cast_specs(to_cast, n_steps):
    specs = [pl.BlockSpec((w.shape[0] // n_steps, w.shape[1]), lambda i: (i, 0)) for w in to_cast]
    shapes = [jax.ShapeDtypeStruct(w.shape, _BF16) for w in to_cast]
    assert all(w.shape[0] % (n_steps * BF16_ROWS) == 0 for w in to_cast)
    return specs, shapes


def _ffn(x2, ada3, g_pre, g_post, w_in, w_out, *, mod_row, seq, to_cast=()):
    m, d = x2.shape
    d_ff = w_out.shape[0]
    tm = FFN_TM
    n = m // tm
    tiles_per_seq = seq // tm
    staged = w_in.dtype == _F32
    cast_specs, cast_shapes = _cast_specs(to_cast, n)
    if staged:
        w_specs = [pl.BlockSpec(memory_space=pl.ANY)] * 2
        scratch = [
            pltpu.VMEM((d, 2 * d_ff), _BF16),
            pltpu.VMEM((d_ff, d), _BF16),
            pltpu.VMEM((W_SLOTS, d // W_CHUNKS, 2 * d_ff), _F32),
            pltpu.VMEM((W_SLOTS, d_ff // W_CHUNKS, d), _F32),
            pltpu.SemaphoreType.DMA((W_SLOTS,)),
            pltpu.SemaphoreType.DMA((W_SLOTS,)),
        ]
    else:
        w_specs = [_resident(w_in.shape), _resident(w_out.shape)]
        scratch = []
    out, *cast = pl.pallas_call(
        functools.partial(_ffn_kernel, mod_row=mod_row, d_ff=d_ff, n_cast=len(to_cast), staged=staged),
        grid=(n,),
        in_specs=[
            pl.BlockSpec((tm, d), lambda i: (i, 0)),
            pl.BlockSpec((1, N_MOD, d), lambda i: (i // tiles_per_seq, 0, 0)),
            _resident((1, d)),
            _resident((1, d)),
            *w_specs,
            *cast_specs,
        ],
        out_specs=[pl.BlockSpec((tm, d), lambda i: (i, 0)), *cast_specs],
        out_shape=[jax.ShapeDtypeStruct((m, d), _F32), *cast_shapes],
        scratch_shapes=scratch,
        compiler_params=pltpu.CompilerParams(
            dimension_semantics=("arbitrary",), vmem_limit_bytes=VMEM_LIMIT),
        name=f"ffn_mod{mod_row}",
    )(x2, ada3, g_pre.reshape(1, d), g_post.reshape(1, d), w_in, w_out, *to_cast)
    return out, cast


def _mixer_kernel(*refs, tiles_per_seq, n_cast):
    (x_ref, ada_ref, gpre_ref, gpost_ref, wmi_ref, lng_ref, lnb_ref, wsp_ref, bsp_ref,
     cw_ref, cb_ref, cng_ref, cnb_ref, goa_ref, gob_ref, wmo_ref) = refs[:16]
    o_ref, gbuf = refs[16 + n_cast], refs[-1]
    _cast_blocks(refs, n_cast, 16)
    tm, d = x_ref.shape
    w_a = lng_ref.shape[1]
    w_b = cw_ref.shape[1]
    n_pairs = w_a // LANES

    @pl.when(pl.program_id(0) % tiles_per_seq == 0)
    def _():
        gbuf[0:CONV_HALO, :] = jnp.zeros((CONV_HALO, w_b), _F32)

    ada = ada_ref[0]
    shift, scale, gate = ada[3:4], ada[4:5], ada[5:6]
    g_in = gpre_ref[...] * (1 + scale)
    g_out = gate * gpost_ref[...]

    t_idx = lax.broadcasted_iota(jnp.int32, (CHUNK, 2 * CHUNK), 0)
    s_idx = lax.broadcasted_iota(jnp.int32, (CHUNK, 2 * CHUNK), 1) % CHUNK
    left = lax.broadcasted_iota(jnp.int32, (CHUNK, LANES), 1) < (LANES // 2)
    w_pairs = [jnp.where(s_idx <= t_idx, wsp_ref[p], 0.0).astype(_BF16) for p in range(n_pairs)]
    first_tap = CONV_HALO - (CONV_K - 1)

    def project(t0):
        x = x_ref[t0:t0 + MIX_SUB, :]
        h = (_rms(x, g_in) + shift).astype(_BF16)
        tile = lambda c0: _dot(h, wmi_ref[:, c0:c0 + MXU_N])
        a_cols, g_cols = 2 * w_a, 2 * w_a + w_b
        ag = [(tile(a_cols + j * MXU_N), tile(g_cols + j * MXU_N)) for j in range(w_b // MXU_N)]
        v = _dot(h, wmi_ref[:, w_a:2 * w_a])
        u = _dot(h, wmi_ref[:, 0:w_a])
        a = jnp.concatenate([t[0] for t in ag], axis=1)
        g = jnp.concatenate([t[1] for t in ag], axis=1)
        return x, (u, v, a, g)

    n_sub = tm // MIX_SUB
    nxt = project(0)
    for sub in range(n_sub):
        t0 = sub * MIX_SUB
        x, proj = nxt
        if sub + 1 < n_sub:
            nxt = project(t0 + MIX_SUB)
        u, v, a, g = proj

        gbuf[CONV_HALO + t0:CONV_HALO + t0 + MIX_SUB, :] = a * jax.nn.sigmoid(g)
        conv_rows = []
        for r in range(MIX_SUB // CONV_ROWS):
            r0 = t0 + r * CONV_ROWS
            conv_cols = []
            for c in range(w_b // LANES):
                cols = slice(c * LANES, (c + 1) * LANES)
                acc = None
                for rho in range(SUBLANES):
                    part = None
                    for k in range(CONV_K):
                        off = first_tap + k
                        if off % SUBLANES != rho:
                            continue
                        lo = r0 + off - rho
                        rows = CONV_ROWS + (SUBLANES if rho else 0)
                        term = gbuf[lo:lo + rows, cols] * cw_ref[k:k + 1, cols]
                        part = term if part is None else part + term
                    part = part[rho:rho + CONV_ROWS]
                    acc = part if acc is None else acc + part
                conv_cols.append(acc)
            conv_rows.append(jnp.concatenate(conv_cols, axis=1))
        conv = jnp.concatenate(conv_rows, axis=0) + cb_ref[...]
        y_b = _rms(_silu(_layer_norm(conv, cng_ref[...], cnb_ref[...])), gob_ref[...])

        v = _layer_norm(v, lng_ref[...], lnb_ref[...])
        z_rows = []
        for n in range(MIX_SUB // CHUNK):
            z_cols = []
            for p in range(n_pairs):
                vp = v[n * CHUNK:(n + 1) * CHUNK, p * LANES:(p + 1) * LANES]
                rhs = jnp.concatenate([jnp.where(left, vp, 0.0), jnp.where(left, 0.0, vp)], axis=0)
                z_cols.append(_dot(w_pairs[p], rhs.astype(_BF16)))
            z_rows.append(jnp.concatenate(z_cols, axis=1) + bsp_ref[...])
        y_a = _rms(u * jnp.concatenate(z_rows, axis=0), goa_ref[...])

        y = jnp.concatenate([y_a, y_b], axis=1).astype(_BF16)
        o_ref[t0:t0 + MIX_SUB, :] = x + _rms(_dot(y, wmo_ref[...]), g_out)

    gbuf[0:CONV_HALO, :] = gbuf[tm:tm + CONV_HALO, :]


def _mixer(x2, ada3, g_pre, g_post, w_mix_in, lng, lnb, w_sp_pairs, b_sp_full, conv_w, conv_b,
           cng, cnb, goa, gob, w_mix_out, *, seq, to_cast=()):
    m, d = x2.shape
    w_a = lng.shape[0]
    w_b = conv_w.shape[1]
    tm = MIX_TM
    tiles_per_seq = seq // tm
    row = lambda v: v.reshape(1, -1)
    cast_specs, cast_shapes = _cast_specs(to_cast, m // tm)
    out, *cast = pl.pallas_call(
        functools.partial(_mixer_kernel, tiles_per_seq=tiles_per_seq, n_cast=len(to_cast)),
        grid=(m // tm,),
        in_specs=[
            pl.BlockSpec((tm, d), lambda i: (i, 0)),
            pl.BlockSpec((1, N_MOD, d), lambda i: (i // tiles_per_seq, 0, 0)),
            _resident((1, d)),
            _resident((1, d)),
            _resident(w_mix_in.shape),
            _resident((1, w_a)),
            _resident((1, w_a)),
            _resident(w_sp_pairs.shape),
            _resident(b_sp_full.shape),
            _resident(conv_w.shape),
            _resident((1, w_b)),
            _resident((1, w_b)),
            _resident((1, w_b)),
            _resident((1, w_a)),
            _resident((1, w_b)),
            _resident(w_mix_out.shape),
            *cast_specs,
        ],
        out_specs=[pl.BlockSpec((tm, d), lambda i: (i, 0)), *cast_specs],
        out_shape=[jax.ShapeDtypeStruct((m, d), _F32), *cast_shapes],
        scratch_shapes=[pltpu.VMEM((CONV_HALO + tm, w_b), _F32)],
        compiler_params=pltpu.CompilerParams(
            dimension_semantics=("arbitrary",), vmem_limit_bytes=VMEM_LIMIT),
        name="mixer",
    )(x2, ada3, row(g_pre), row(g_post), w_mix_in, row(lng), row(lnb), w_sp_pairs, b_sp_full,
      conv_w, row(conv_b), row(cng), row(cnb), row(goa), row(gob), w_mix_out, *to_cast)
    return out, cast


def kernel(x, c, w_ada, b_ada, g_pre_f1, g_post_f1, w_f1_in, w_f1_out, g_pre_m, g_post_m, w_mix_in,
           gmlp_norm_g, gmlp_norm_b, w_spatial, b_spatial, conv_w, conv_b, conv_norm_g, conv_norm_b,
           g_out_a, g_out_b, w_mix_out, g_pre_f2, g_post_f2, w_f2_in, w_f2_out):
    bsz, seq, d = x.shape
    depth = w_ada.shape[0]
    w_a = gmlp_norm_g.shape[1]
    hd_a = w_a // H_A
    assert seq % FFN_TM == 0 and FFN_TM % FFN_SUB == 0 and (N_MOD * d) % (2 * ADA_TN) == 0
    assert d % (W_CHUNKS * SUBLANES) == 0 and w_f1_out.shape[1] % (W_CHUNKS * SUBLANES) == 0
    assert seq % MIX_TM == 0 and MIX_TM % MIX_SUB == 0 and MIX_SUB % CHUNK == 0
    assert 2 * hd_a == LANES and w_spatial.shape[2:] == (CHUNK, CHUNK)
    assert CONV_HALO >= CONV_K - 1 and MIX_SUB % CONV_ROWS == 0

    x2 = x.reshape(bsz * seq, d)
    for l in range(depth):
        ada3 = _ada(c, w_ada[l], b_ada[l]).reshape(bsz, N_MOD, d)
        x2, (w_mi, w_mo) = _ffn(x2, ada3, g_pre_f1[l], g_post_f1[l], w_f1_in[l], w_f1_out[l], mod_row=0,
                                seq=seq, to_cast=(w_mix_in[l], w_mix_out[l]))
        w_sp_pairs = w_spatial[l].reshape(H_A // 2, 2, CHUNK, CHUNK).transpose(0, 2, 1, 3)
        w_sp_pairs = w_sp_pairs.reshape(H_A // 2, CHUNK, 2 * CHUNK)
        b_sp_full = jnp.repeat(b_spatial[l].T, hd_a, axis=1)
        d_ff = w_f2_out.shape[1]
        x2, (w2_in, w2_out) = _mixer(x2, ada3, g_pre_m[l], g_post_m[l], w_mi, gmlp_norm_g[l],
                                     gmlp_norm_b[l], w_sp_pairs, b_sp_full, conv_w[l], conv_b[l],
                                     conv_norm_g[l], conv_norm_b[l], g_out_a[l], g_out_b[l], w_mo, seq=seq,
                                     to_cast=(w_f2_in[l], w_f2_out[l].reshape(d, d_ff)))
        x2, _ = _ffn(x2, ada3, g_pre_f2[l], g_post_f2[l], w2_in, w2_out.reshape(d_ff, d), mod_row=6, seq=seq)
    return x2.reshape(bsz, seq, d)
```

```python
import functools

import jax
import jax.numpy as jnp
from jax import lax
from jax.experimental import pallas as pl
from jax.experimental.pallas import tpu as pltpu

N_MOD = 9
H_A = 8
CHUNK = 128
CONV_K = 31
EPS = 1e-6
HALF = 0.5

CONV_HALO = 32
CONV_ROWS = 64
LANES = 128
SUBLANES = 8
MXU_N = 256
BF16_ROWS = 16
ANCHOR_FROM = 3

FFN_TM = 1024
FFN_SUB = 256
MIX_TM = 512
MIX_SUB = 256
ADA_TN = 1536
W_CHUNKS = 32
W_SLOTS = 8
VMEM_LIMIT = 56 * 1024 * 1024

_F32 = jnp.float32
_BF16 = jnp.bfloat16


def _rms(x, g):
    return x * lax.rsqrt(jnp.mean(x * x, axis=-1, keepdims=True) + EPS) * g


def _layer_norm(x, g, b):
    mu = jnp.mean(x, axis=-1, keepdims=True)
    xc = x - mu
    var = jnp.mean(xc * xc, axis=-1, keepdims=True)
    return xc * lax.rsqrt(var + EPS) * g + b


def _silu(x):
    return x * jax.nn.sigmoid(x)


def _dot(a, b):
    return jnp.dot(a, b, preferred_element_type=_F32)


def _anchored(h, anchor):
    bits = lax.bitcast_convert_type(anchor.astype(_F32), jnp.uint32)
    zero = lax.shift_right_logical(lax.shift_right_logical(bits, jnp.uint32(16)), jnp.uint32(16))
    zero = lax.bitcast_convert_type(zero, _F32).astype(h.dtype)
    r, c = anchor.shape
    top = jnp.concatenate([h[0:r, 0:c] + zero, h[0:r, c:]], axis=1)
    return jnp.concatenate([top, h[r:]], axis=0)


def _ada_kernel(c_ref, wlo_ref, whi_ref, blo_ref, bhi_ref, olo_ref, ohi_ref):
    s = _silu(c_ref[...]).astype(_BF16)
    olo_ref[...] = _dot(s, wlo_ref[...].astype(_BF16)) + blo_ref[...]
    ohi_ref[...] = _dot(s, whi_ref[...].astype(_BF16)) + bhi_ref[...]


def _ada(c, w, b):
    bsz, d = c.shape
    n = w.shape[1]
    nb = n // (2 * ADA_TN)
    b = b.reshape(1, n)
    lo = lambda j: (0, j)
    hi = lambda j: (0, j + nb)
    out_lo, out_hi = pl.pallas_call(
        _ada_kernel,
        grid=(nb,),
        in_specs=[
            pl.BlockSpec((bsz, d), lambda j: (0, 0)),
            pl.BlockSpec((d, ADA_TN), lo),
            pl.BlockSpec((d, ADA_TN), hi),
            pl.BlockSpec((1, ADA_TN), lo),
            pl.BlockSpec((1, ADA_TN), hi),
        ],
        out_specs=[pl.BlockSpec((bsz, ADA_TN), lo), pl.BlockSpec((bsz, ADA_TN), lo)],
        out_shape=[jax.ShapeDtypeStruct((bsz, n // 2), _F32)] * 2,
        compiler_params=pltpu.CompilerParams(
            dimension_semantics=("parallel",), vmem_limit_bytes=VMEM_LIMIT),
        name="ada_proj",
    )(c, w, w, b, b)
    return jnp.concatenate([out_lo, out_hi], axis=1)


def _stage_bf16(src_hbm, dst_ref, stage_ref, sems):
    n_slots, rows, _ = stage_ref.shape
    n_chunks = src_hbm.shape[0] // rows

    def copy(i):
        slot = i % n_slots
        return pltpu.make_async_copy(src_hbm.at[pl.ds(i * rows, rows), :], stage_ref.at[slot], sems.at[slot])

    for i in range(min(n_slots - 1, n_chunks)):
        copy(i).start()
    for i in range(n_chunks):
        if i + n_slots - 1 < n_chunks:
            copy(i + n_slots - 1).start()
        copy(i).wait()
        dst_ref[i * rows:(i + 1) * rows, :] = stage_ref[i % n_slots].astype(_BF16)


def _cast_blocks(refs, n_cast, n_before):
    srcs = refs[n_before:n_before + n_cast]
    dsts = refs[n_before + n_cast + 1:n_before + 2 * n_cast + 1]
    for src, dst in zip(srcs, dsts):
        dst[...] = src[...].astype(_BF16)


def _ffn_kernel(*refs, mod_row, d_ff, n_cast, staged):
    x_ref, ada_ref, gpre_ref, gpost_ref, win_ref, wout_ref = refs[:6]
    o_ref = refs[6 + n_cast]
    _cast_blocks(refs, n_cast, 6)
    if staged:
        win_hbm, wout_hbm = win_ref, wout_ref
        win_ref, wout_ref, stage_in, stage_out, sem_in, sem_out = refs[7 + 2 * n_cast:]

        @pl.when(pl.program_id(0) == 0)
        def _():
            _stage_bf16(win_hbm, win_ref, stage_in, sem_in)
            _stage_bf16(wout_hbm, wout_ref, stage_out, sem_out)

    ada = ada_ref[0]
    shift = ada[mod_row:mod_row + 1]
    scale = ada[mod_row + 1:mod_row + 2]
    gate = ada[mod_row + 2:mod_row + 3]
    g_in = gpre_ref[...] * (1 + scale)
    g_out = (HALF * gate) * gpost_ref[...]
    n_sub = x_ref.shape[0] // FFN_SUB
    n_tiles = d_ff // MXU_N
    n_groups = FFN_SUB // BF16_ROWS
    rows = [slice(s * FFN_SUB, (s + 1) * FFN_SUB) for s in range(n_sub)]
    hs = [(_rms(x_ref[r, :], g_in) + shift).astype(_BF16) for r in rows]
    for s in range(n_sub):
        x = x_ref[rows[s], :]
        act = []
        for j in range(n_tiles):
            h = hs[s]
            if s + 1 < n_sub and j >= ANCHOR_FROM:
                for k in range(j - ANCHOR_FROM, n_groups, n_tiles - ANCHOR_FROM):
                    h = _anchored(h, hs[s + 1][k * BF16_ROWS:(k + 1) * BF16_ROWS, 0:LANES])
            g = _dot(h, win_ref[:, j * MXU_N:(j + 1) * MXU_N])
            u = _dot(h, win_ref[:, d_ff + j * MXU_N:d_ff + (j + 1) * MXU_N])
            act.append(_silu(g) * u)
        act = jnp.concatenate(act, axis=1).astype(_BF16)
        y = _dot(act, wout_ref[...])
        o_ref[rows[s], :] = x + _rms(y, g_out)


def _resident(shape):
    return pl.BlockSpec(shape, lambda i: (0,) * len(shape), pipeline_mode=pl.Buffered(1))


def _cast_specs(to_cast, n_steps):
    specs = []
    for w in to_cast:
        rows = next(r for r in range(BF16_ROWS, w.shape[0] + 1, BF16_ROWS)
                    if w.shape[0] % r == 0 and w.shape[0] // r <= n_steps)
        specs.append(pl.BlockSpec((rows, w.shape[1]), functools.partial(
            lambda i, last: (jnp.minimum(i, last), 0), last=w.shape[0] // rows - 1)))
    shapes = [jax.ShapeDtypeStruct(w.shape, _BF16) for w in to_cast]
    return specs, shapes


def _ffn(x2, ada3, g_pre, g_post, w_in, w_out, *, mod_row, seq, to_cast=()):
    m, d = x2.shape
    d_ff = w_out.shape[0]
    tm = FFN_TM
    n = m // tm
    tiles_per_seq = seq // tm
    staged = w_in.dtype == _F32
    cast_specs, cast_shapes = _cast_specs(to_cast, n)
    if staged:
        w_specs = [pl.BlockSpec(memory_space=pl.ANY)] * 2
        scratch = [
            pltpu.VMEM((d, 2 * d_ff), _BF16),
            pltpu.VMEM((d_ff, d), _BF16),
            pltpu.VMEM((W_SLOTS, d // W_CHUNKS, 2 * d_ff), _F32),
            pltpu.VMEM((W_SLOTS, d_ff // W_CHUNKS, d), _F32),
            pltpu.SemaphoreType.DMA((W_SLOTS,)),
            pltpu.SemaphoreType.DMA((W_SLOTS,)),
        ]
    else:
        w_specs = [_resident(w_in.shape), _resident(w_out.shape)]
        scratch = []
    out, *cast = pl.pallas_call(
        functools.partial(_ffn_kernel, mod_row=mod_row, d_ff=d_ff, n_cast=len(to_cast), staged=staged),
        grid=(n,),
        in_specs=[
            pl.BlockSpec((tm, d), lambda i: (i, 0)),
            pl.BlockSpec((1, N_MOD, d), lambda i: (i // tiles_per_seq, 0, 0)),
            _resident((1, d)),
            _resident((1, d)),
            *w_specs,
            *cast_specs,
        ],
        out_specs=[pl.BlockSpec((tm, d), lambda i: (i, 0)), *cast_specs],
        out_shape=[jax.ShapeDtypeStruct((m, d), _F32), *cast_shapes],
        scratch_shapes=scratch,
        compiler_params=pltpu.CompilerParams(
            dimension_semantics=("arbitrary",), vmem_limit_bytes=VMEM_LIMIT),
        name=f"ffn_mod{mod_row}",
    )(x2, ada3, g_pre.reshape(1, d), g_post.reshape(1, d), w_in, w_out, *to_cast)
    return out, cast


def _mixer_kernel(*refs, tiles_per_seq, n_cast):
    (x_ref, ada_ref, gpre_ref, gpost_ref, wmi_ref, lng_ref, lnb_ref, wsp_ref, bsp_ref,
     cw_ref, cb_ref, cng_ref, cnb_ref, goa_ref, gob_ref, wmo_ref) = refs[:16]
    o_ref, gbuf = refs[16 + n_cast], refs[-1]
    _cast_blocks(refs, n_cast, 16)
    tm, d = x_ref.shape
    w_a = lng_ref.shape[1]
    w_b = cw_ref.shape[1]
    n_pairs = w_a // LANES

    @pl.when(pl.program_id(0) % tiles_per_seq == 0)
    def _():
        gbuf[0:CONV_HALO, :] = jnp.zeros((CONV_HALO, w_b), _F32)

    ada = ada_ref[0]
    shift, scale, gate = ada[3:4], ada[4:5], ada[5:6]
    g_in = gpre_ref[...] * (1 + scale)
    g_out = gate * gpost_ref[...]

    t_idx = lax.broadcasted_iota(jnp.int32, (CHUNK, 2 * CHUNK), 0)
    s_idx = lax.broadcasted_iota(jnp.int32, (CHUNK, 2 * CHUNK), 1) % CHUNK
    left = lax.broadcasted_iota(jnp.int32, (CHUNK, LANES), 1) < (LANES // 2)
    w_pairs = [jnp.where(s_idx <= t_idx, wsp_ref[p], 0.0).astype(_BF16) for p in range(n_pairs)]
    first_tap = CONV_HALO - (CONV_K - 1)

    def project(t0):
        x = x_ref[t0:t0 + MIX_SUB, :]
        h = (_rms(x, g_in) + shift).astype(_BF16)
        tile = lambda c0: _dot(h, wmi_ref[:, c0:c0 + MXU_N])
        a_cols, g_cols = 2 * w_a, 2 * w_a + w_b
        ag = [(tile(a_cols + j * MXU_N), tile(g_cols + j * MXU_N)) for j in range(w_b // MXU_N)]
        v = _dot(h, wmi_ref[:, w_a:2 * w_a])
        u = _dot(h, wmi_ref[:, 0:w_a])
        a = jnp.concatenate([t[0] for t in ag], axis=1)
        g = jnp.concatenate([t[1] for t in ag], axis=1)
        return x, (u, v, a, g)

    n_sub = tm // MIX_SUB
    nxt = project(0)
    for sub in range(n_sub):
        t0 = sub * MIX_SUB
        x, proj = nxt
        if sub + 1 < n_sub:
            nxt = project(t0 + MIX_SUB)
        u, v, a, g = proj

        gbuf[CONV_HALO + t0:CONV_HALO + t0 + MIX_SUB, :] = a * jax.nn.sigmoid(g)
        conv_rows = []
        for r in range(MIX_SUB // CONV_ROWS):
            r0 = t0 + r * CONV_ROWS
            conv_cols = []
            for c in range(w_b // LANES):
                cols = slice(c * LANES, (c + 1) * LANES)
                acc = None
                for rho in range(SUBLANES):
                    part = None
                    for k in range(CONV_K):
                        off = first_tap + k
                        if off % SUBLANES != rho:
                            continue
                        lo = r0 + off - rho
                        rows = CONV_ROWS + (SUBLANES if rho else 0)
                        term = gbuf[lo:lo + rows, cols] * cw_ref[k:k + 1, cols]
                        part = term if part is None else part + term
                    part = part[rho:rho + CONV_ROWS]
                    acc = part if acc is None else acc + part
                conv_cols.append(acc)
            conv_rows.append(jnp.concatenate(conv_cols, axis=1))
        conv = jnp.concatenate(conv_rows, axis=0) + cb_ref[...]
        y_b = _rms(_silu(_layer_norm(conv, cng_ref[...], cnb_ref[...])), gob_ref[...])

        v = _layer_norm(v, lng_ref[...], lnb_ref[...])
        z_rows = []
        for n in range(MIX_SUB // CHUNK):
            z_cols = []
            for p in range(n_pairs):
                vp = v[n * CHUNK:(n + 1) * CHUNK, p * LANES:(p + 1) * LANES]
                rhs = jnp.concatenate([jnp.where(left, vp, 0.0), jnp.where(left, 0.0, vp)], axis=0)
                z_cols.append(_dot(w_pairs[p], rhs.astype(_BF16)))
            z_rows.append(jnp.concatenate(z_cols, axis=1) + bsp_ref[...])
        y_a = _rms(u * jnp.concatenate(z_rows, axis=0), goa_ref[...])

        y = jnp.concatenate([y_a, y_b], axis=1).astype(_BF16)
        o_ref[t0:t0 + MIX_SUB, :] = x + _rms(_dot(y, wmo_ref[...]), g_out)

    gbuf[0:CONV_HALO, :] = gbuf[tm:tm + CONV_HALO, :]


def _mixer(x2, ada3, g_pre, g_post, w_mix_in, lng, lnb, w_sp_pairs, b_sp_full, conv_w, conv_b,
           cng, cnb, goa, gob, w_mix_out, *, seq, to_cast=()):
    m, d = x2.shape
    w_a = lng.shape[0]
    w_b = conv_w.shape[1]
    tm = MIX_TM
    tiles_per_seq = seq // tm
    row = lambda v: v.reshape(1, -1)
    cast_specs, cast_shapes = _cast_specs(to_cast, m // tm)
    out, *cast = pl.pallas_call(
        functools.partial(_mixer_kernel, tiles_per_seq=tiles_per_seq, n_cast=len(to_cast)),
        grid=(m // tm,),
        in_specs=[
            pl.BlockSpec((tm, d), lambda i: (i, 0)),
            pl.BlockSpec((1, N_MOD, d), lambda i: (i // tiles_per_seq, 0, 0)),
            _resident((1, d)),
            _resident((1, d)),
            _resident(w_mix_in.shape),
            _resident((1, w_a)),
            _resident((1, w_a)),
            _resident(w_sp_pairs.shape),
            _resident(b_sp_full.shape),
            _resident(conv_w.shape),
            _resident((1, w_b)),
            _resident((1, w_b)),
            _resident((1, w_b)),
            _resident((1, w_a)),
            _resident((1, w_b)),
            _resident(w_mix_out.shape),
            *cast_specs,
        ],
        out_specs=[pl.BlockSpec((tm, d), lambda i: (i, 0)), *cast_specs],
        out_shape=[jax.ShapeDtypeStruct((m, d), _F32), *cast_shapes],
        scratch_shapes=[pltpu.VMEM((CONV_HALO + tm, w_b), _F32)],
        compiler_params=pltpu.CompilerParams(
            dimension_semantics=("arbitrary",), vmem_limit_bytes=VMEM_LIMIT),
        name="mixer",
    )(x2, ada3, row(g_pre), row(g_post), w_mix_in, row(lng), row(lnb), w_sp_pairs, b_sp_full,
      conv_w, row(conv_b), row(cng), row(cnb), row(goa), row(gob), w_mix_out, *to_cast)
    return out, cast


def kernel(x, c, w_ada, b_ada, g_pre_f1, g_post_f1, w_f1_in, w_f1_out, g_pre_m, g_post_m, w_mix_in,
           gmlp_norm_g, gmlp_norm_b, w_spatial, b_spatial, conv_w, conv_b, conv_norm_g, conv_norm_b,
           g_out_a, g_out_b, w_mix_out, g_pre_f2, g_post_f2, w_f2_in, w_f2_out):
    bsz, seq, d = x.shape
    depth = w_ada.shape[0]
    w_a = gmlp_norm_g.shape[1]
    hd_a = w_a // H_A
    assert seq % FFN_TM == 0 and FFN_TM % FFN_SUB == 0 and (N_MOD * d) % (2 * ADA_TN) == 0
    assert d % (W_CHUNKS * SUBLANES) == 0 and w_f1_out.shape[1] % (W_CHUNKS * SUBLANES) == 0
    assert seq % MIX_TM == 0 and MIX_TM % MIX_SUB == 0 and MIX_SUB % CHUNK == 0
    assert 2 * hd_a == LANES and w_spatial.shape[2:] == (CHUNK, CHUNK)
    assert CONV_HALO >= CONV_K - 1 and MIX_SUB % CONV_ROWS == 0

    x2 = x.reshape(bsz * seq, d)
    for l in range(depth):
        ada3 = _ada(c, w_ada[l], b_ada[l]).reshape(bsz, N_MOD, d)
        x2, _ = _ffn(x2, ada3, g_pre_f1[l], g_post_f1[l], w_f1_in[l], w_f1_out[l], mod_row=0, seq=seq)
        w_sp_pairs = w_spatial[l].reshape(H_A // 2, 2, CHUNK, CHUNK).transpose(0, 2, 1, 3)
        w_sp_pairs = w_sp_pairs.reshape(H_A // 2, CHUNK, 2 * CHUNK)
        b_sp_full = jnp.repeat(b_spatial[l].T, hd_a, axis=1)
        x2, (w2_in, w2_out) = _mixer(x2, ada3, g_pre_m[l], g_post_m[l], w_mix_in[l].astype(_BF16),
                                     gmlp_norm_g[l], gmlp_norm_b[l], w_sp_pairs, b_sp_full, conv_w[l],
                                     conv_b[l], conv_norm_g[l], conv_norm_b[l], g_out_a[l], g_out_b[l],
                                     w_mix_out[l].astype(_BF16), seq=seq, to_cast=(w_f2_in[l], w_f2_out[l]))
        x2, _ = _ffn(x2, ada3, g_pre_f2[l], g_post_f2[l], w2_in, w2_out, mod_row=6, seq=seq)
    return x2.reshape(bsz, seq, d)
```

```python
import functools

import jax
import jax.numpy as jnp
from jax import lax
from jax.experimental import pallas as pl
from jax.experimental.pallas import tpu as pltpu

N_MOD = 9
H_A = 8
CHUNK = 128
CONV_K = 31
EPS = 1e-6
HALF = 0.5

CONV_HALO = 32
CONV_ROWS = 64
LANES = 128
SUBLANES = 8
MXU_N = 256
BF16_ROWS = 16
ANCHOR_FROM = 3

FFN_TM = 1024
FFN_SUB = 256
MIX_TM = 512
MIX_SUB = 256
ADA_TN = 1536
W_CHUNKS = 32
W_SLOTS = 8
VMEM_LIMIT = 56 * 1024 * 1024

_F32 = jnp.float32
_BF16 = jnp.bfloat16


def _rms(x, g):
    return x * lax.rsqrt(jnp.mean(x * x, axis=-1, keepdims=True) + EPS) * g


def _layer_norm(x, g, b):
    mu = jnp.mean(x, axis=-1, keepdims=True)
    xc = x - mu
    var = jnp.mean(xc * xc, axis=-1, keepdims=True)
    return xc * lax.rsqrt(var + EPS) * g + b


def _silu(x):
    return x * jax.nn.sigmoid(x)


def _dot(a, b):
    return jnp.dot(a, b, preferred_element_type=_F32)


def _anchored(h, anchor):
    bits = lax.bitcast_convert_type(anchor.astype(_F32), jnp.uint32)
    zero = lax.shift_right_logical(lax.shift_right_logical(bits, jnp.uint32(16)), jnp.uint32(16))
    zero = lax.bitcast_convert_type(zero, _F32).astype(h.dtype)
    r, c = anchor.shape
    top = jnp.concatenate([h[0:r, 0:c] + zero, h[0:r, c:]], axis=1)
    return jnp.concatenate([top, h[r:]], axis=0)


def _ada_kernel(c_ref, wlo_ref, whi_ref, blo_ref, bhi_ref, olo_ref, ohi_ref):
    s = _silu(c_ref[...]).astype(_BF16)
    olo_ref[...] = _dot(s, wlo_ref[...].astype(_BF16)) + blo_ref[...]
    ohi_ref[...] = _dot(s, whi_ref[...].astype(_BF16)) + bhi_ref[...]


def _ada(c, w, b):
    bsz, d = c.shape
    n = w.shape[1]
    nb = n // (2 * ADA_TN)
    b = b.reshape(1, n)
    lo = lambda j: (0, j)
    hi = lambda j: (0, j + nb)
    out_lo, out_hi = pl.pallas_call(
        _ada_kernel,
        grid=(nb,),
        in_specs=[
            pl.BlockSpec((bsz, d), lambda j: (0, 0)),
            pl.BlockSpec((d, ADA_TN), lo),
            pl.BlockSpec((d, ADA_TN), hi),
            pl.BlockSpec((1, ADA_TN), lo),
            pl.BlockSpec((1, ADA_TN), hi),
        ],
        out_specs=[pl.BlockSpec((bsz, ADA_TN), lo), pl.BlockSpec((bsz, ADA_TN), lo)],
        out_shape=[jax.ShapeDtypeStruct((bsz, n // 2), _F32)] * 2,
        compiler_params=pltpu.CompilerParams(
            dimension_semantics=("parallel",), vmem_limit_bytes=VMEM_LIMIT),
        name="ada_proj",
    )(c, w, w, b, b)
    return jnp.concatenate([out_lo, out_hi], axis=1)


def _stage_bf16(src_hbm, dst_ref, stage_ref, sems):
    n_slots, rows, _ = stage_ref.shape
    n_chunks = src_hbm.shape[0] // rows

    def copy(i):
        slot = i % n_slots
        return pltpu.make_async_copy(src_hbm.at[pl.ds(i * rows, rows), :], stage_ref.at[slot], sems.at[slot])

    for i in range(min(n_slots - 1, n_chunks)):
        copy(i).start(priority=i % 2)
    for i in range(n_chunks):
        if i + n_slots - 1 < n_chunks:
            copy(i + n_slots - 1).start(priority=(i + n_slots - 1) % 2)
        copy(i).wait()
        dst_ref[i * rows:(i + 1) * rows, :] = stage_ref[i % n_slots].astype(_BF16)


def _ffn_kernel(x_ref, ada_ref, gpre_ref, gpost_ref, win_hbm, wout_hbm, o_ref,
                win_ref, wout_ref, stage_in, stage_out, sem_in, sem_out, *, mod_row, d_ff):
    @pl.when(pl.program_id(0) == 0)
    def _():
        _stage_bf16(win_hbm, win_ref, stage_in, sem_in)
        _stage_bf16(wout_hbm, wout_ref, stage_out, sem_out)

    ada = ada_ref[0]
    shift = ada[mod_row:mod_row + 1]
    scale = ada[mod_row + 1:mod_row + 2]
    gate = ada[mod_row + 2:mod_row + 3]
    g_in = gpre_ref[...] * (1 + scale)
    g_out = (HALF * gate) * gpost_ref[...]
    n_sub = x_ref.shape[0] // FFN_SUB
    n_tiles = d_ff // MXU_N
    n_groups = FFN_SUB // BF16_ROWS
    rows = [slice(s * FFN_SUB, (s + 1) * FFN_SUB) for s in range(n_sub)]
    hs = [(_rms(x_ref[r, :], g_in) + shift).astype(_BF16) for r in rows]
    for s in range(n_sub):
        x = x_ref[rows[s], :]
        act = []
        for j in range(n_tiles):
            h = hs[s]
            if s + 1 < n_sub and j >= ANCHOR_FROM:
                for k in range(j - ANCHOR_FROM, n_groups, n_tiles - ANCHOR_FROM):
                    h = _anchored(h, hs[s + 1][k * BF16_ROWS:(k + 1) * BF16_ROWS, 0:LANES])
            g = _dot(h, win_ref[:, j * MXU_N:(j + 1) * MXU_N])
            u = _dot(h, win_ref[:, d_ff + j * MXU_N:d_ff + (j + 1) * MXU_N])
            act.append(_silu(g) * u)
        act = jnp.concatenate(act, axis=1).astype(_BF16)
        y = _dot(act, wout_ref[...])
        o_ref[rows[s], :] = x + _rms(y, g_out)


def _resident(shape):
    return pl.BlockSpec(shape, lambda i: (0,) * len(shape), pipeline_mode=pl.Buffered(1))


def _ffn(x2, ada3, g_pre, g_post, w_in, w_out, *, mod_row, seq):
    m, d = x2.shape
    d_ff = w_out.shape[0]
    tm = FFN_TM
    tiles_per_seq = seq // tm
    return pl.pallas_call(
        functools.partial(_ffn_kernel, mod_row=mod_row, d_ff=d_ff),
        grid=(m // tm,),
        in_specs=[
            pl.BlockSpec((tm, d), lambda i: (i, 0)),
            pl.BlockSpec((1, N_MOD, d), lambda i: (i // tiles_per_seq, 0, 0)),
            _resident((1, d)),
            _resident((1, d)),
            pl.BlockSpec(memory_space=pl.ANY),
            pl.BlockSpec(memory_space=pl.ANY),
        ],
        out_specs=pl.BlockSpec((tm, d), lambda i: (i, 0)),
        out_shape=jax.ShapeDtypeStruct((m, d), _F32),
        scratch_shapes=[
            pltpu.VMEM((d, 2 * d_ff), _BF16),
            pltpu.VMEM((d_ff, d), _BF16),
            pltpu.VMEM((W_SLOTS, d // W_CHUNKS, 2 * d_ff), _F32),
            pltpu.VMEM((W_SLOTS, d_ff // W_CHUNKS, d), _F32),
            pltpu.SemaphoreType.DMA((W_SLOTS,)),
            pltpu.SemaphoreType.DMA((W_SLOTS,)),
        ],
        compiler_params=pltpu.CompilerParams(
            dimension_semantics=("arbitrary",), vmem_limit_bytes=VMEM_LIMIT),
        name=f"ffn_mod{mod_row}",
    )(x2, ada3, g_pre.reshape(1, d), g_post.reshape(1, d), w_in, w_out)


def _mixer_kernel(x_ref, ada_ref, gpre_ref, gpost_ref, wmi_ref, lng_ref, lnb_ref, wsp_ref, bsp_ref,
                  cw_ref, cb_ref, cng_ref, cnb_ref, goa_ref, gob_ref, wmo_ref, o_ref, gbuf,
                  *, tiles_per_seq):
    tm, d = x_ref.shape
    w_a = lng_ref.shape[1]
    w_b = cw_ref.shape[1]
    n_pairs = w_a // LANES

    @pl.when(pl.program_id(0) % tiles_per_seq == 0)
    def _():
        gbuf[0:CONV_HALO, :] = jnp.zeros((CONV_HALO, w_b), _F32)

    ada = ada_ref[0]
    shift, scale, gate = ada[3:4], ada[4:5], ada[5:6]
    g_in = gpre_ref[...] * (1 + scale)
    g_out = gate * gpost_ref[...]

    t_idx = lax.broadcasted_iota(jnp.int32, (CHUNK, 2 * CHUNK), 0)
    s_idx = lax.broadcasted_iota(jnp.int32, (CHUNK, 2 * CHUNK), 1) % CHUNK
    left = lax.broadcasted_iota(jnp.int32, (CHUNK, LANES), 1) < (LANES // 2)
    w_pairs = [jnp.where(s_idx <= t_idx, wsp_ref[p], 0.0).astype(_BF16) for p in range(n_pairs)]
    first_tap = CONV_HALO - (CONV_K - 1)

    def project(t0):
        x = x_ref[t0:t0 + MIX_SUB, :]
        h = (_rms(x, g_in) + shift).astype(_BF16)
        tile = lambda c0: _dot(h, wmi_ref[:, c0:c0 + MXU_N])
        a_cols, g_cols = 2 * w_a, 2 * w_a + w_b
        ag = [(tile(a_cols + j * MXU_N), tile(g_cols + j * MXU_N)) for j in range(w_b // MXU_N)]
        v = _dot(h, wmi_ref[:, w_a:2 * w_a])
        u = _dot(h, wmi_ref[:, 0:w_a])
        a = jnp.concatenate([t[0] for t in ag], axis=1)
        g = jnp.concatenate([t[1] for t in ag], axis=1)
        return x, (u, v, a, g)

    n_sub = tm // MIX_SUB
    nxt = project(0)
    for sub in range(n_sub):
        t0 = sub * MIX_SUB
        x, proj = nxt
        if sub + 1 < n_sub:
            nxt = project(t0 + MIX_SUB)
        u, v, a, g = proj

        gbuf[CONV_HALO + t0:CONV_HALO + t0 + MIX_SUB, :] = a * jax.nn.sigmoid(g)
        conv_rows = []
        for r in range(MIX_SUB // CONV_ROWS):
            r0 = t0 + r * CONV_ROWS
            conv_cols = []
            for c in range(w_b // LANES):
                cols = slice(c * LANES, (c + 1) * LANES)
                acc = None
                for rho in range(SUBLANES):
                    part = None
                    for k in range(CONV_K):
                        off = first_tap + k
                        if off % SUBLANES != rho:
                            continue
                        lo = r0 + off - rho
                        rows = CONV_ROWS + (SUBLANES if rho else 0)
                        term = gbuf[lo:lo + rows, cols] * cw_ref[k:k + 1, cols]
                        part = term if part is None else part + term
                    part = part[rho:rho + CONV_ROWS]
                    acc = part if acc is None else acc + part
                conv_cols.append(acc)
            conv_rows.append(jnp.concatenate(conv_cols, axis=1))
        conv = jnp.concatenate(conv_rows, axis=0) + cb_ref[...]
        y_b = _rms(_silu(_layer_norm(conv, cng_ref[...], cnb_ref[...])), gob_ref[...])

        v = _layer_norm(v, lng_ref[...], lnb_ref[...])
        z_rows = []
        for n in range(MIX_SUB // CHUNK):
            z_cols = []
            for p in range(n_pairs):
                vp = v[n * CHUNK:(n + 1) * CHUNK, p * LANES:(p + 1) * LANES]
                rhs = jnp.concatenate([jnp.where(left, vp, 0.0), jnp.where(left, 0.0, vp)], axis=0)
                z_cols.append(_dot(w_pairs[p], rhs.astype(_BF16)))
            z_rows.append(jnp.concatenate(z_cols, axis=1) + bsp_ref[...])
        y_a = _rms(u * jnp.concatenate(z_rows, axis=0), goa_ref[...])

        y = jnp.concatenate([y_a, y_b], axis=1).astype(_BF16)
        o_ref[t0:t0 + MIX_SUB, :] = x + _rms(_dot(y, wmo_ref[...]), g_out)

    gbuf[0:CONV_HALO, :] = gbuf[tm:tm + CONV_HALO, :]


def _mixer(x2, ada3, g_pre, g_post, w_mix_in, lng, lnb, w_sp_pairs, b_sp_full, conv_w, conv_b,
           cng, cnb, goa, gob, w_mix_out, *, seq):
    m, d = x2.shape
    w_a = lng.shape[0]
    w_b = conv_w.shape[1]
    tm = MIX_TM
    tiles_per_seq = seq // tm
    row = lambda v: v.reshape(1, -1)
    return pl.pallas_call(
        functools.partial(_mixer_kernel, tiles_per_seq=tiles_per_seq),
        grid=(m // tm,),
        in_specs=[
            pl.BlockSpec((tm, d), lambda i: (i, 0)),
            pl.BlockSpec((1, N_MOD, d), lambda i: (i // tiles_per_seq, 0, 0)),
            _resident((1, d)),
            _resident((1, d)),
            _resident(w_mix_in.shape),
            _resident((1, w_a)),
            _resident((1, w_a)),
            _resident(w_sp_pairs.shape),
            _resident(b_sp_full.shape),
            _resident(conv_w.shape),
            _resident((1, w_b)),
            _resident((1, w_b)),
            _resident((1, w_b)),
            _resident((1, w_a)),
            _resident((1, w_b)),
            _resident(w_mix_out.shape),
        ],
        out_specs=pl.BlockSpec((tm, d), lambda i: (i, 0)),
        out_shape=jax.ShapeDtypeStruct((m, d), _F32),
        scratch_shapes=[pltpu.VMEM((CONV_HALO + tm, w_b), _F32)],
        compiler_params=pltpu.CompilerParams(
            dimension_semantics=("arbitrary",), vmem_limit_bytes=VMEM_LIMIT),
        name="mixer",
    )(x2, ada3, row(g_pre), row(g_post), w_mix_in, row(lng), row(lnb), w_sp_pairs, b_sp_full,
      conv_w, row(conv_b), row(cng), row(cnb), row(goa), row(gob), w_mix_out)


def kernel(x, c, w_ada, b_ada, g_pre_f1, g_post_f1, w_f1_in, w_f1_out, g_pre_m, g_post_m, w_mix_in,
           gmlp_norm_g, gmlp_norm_b, w_spatial, b_spatial, conv_w, conv_b, conv_norm_g, conv_norm_b,
           g_out_a, g_out_b, w_mix_out, g_pre_f2, g_post_f2, w_f2_in, w_f2_out):
    bsz, seq, d = x.shape
    depth = w_ada.shape[0]
    w_a = gmlp_norm_g.shape[1]
    hd_a = w_a // H_A
    assert seq % FFN_TM == 0 and FFN_TM % FFN_SUB == 0 and (N_MOD * d) % (2 * ADA_TN) == 0
    assert d % (W_CHUNKS * SUBLANES) == 0 and w_f1_out.shape[1] % (W_CHUNKS * SUBLANES) == 0
    assert seq % MIX_TM == 0 and MIX_TM % MIX_SUB == 0 and MIX_SUB % CHUNK == 0
    assert 2 * hd_a == LANES and w_spatial.shape[2:] == (CHUNK, CHUNK)
    assert CONV_HALO >= CONV_K - 1 and MIX_SUB % CONV_ROWS == 0

    x2 = x.reshape(bsz * seq, d)
    for l in range(depth):
        ada3 = _ada(c, w_ada[l], b_ada[l]).reshape(bsz, N_MOD, d)
        x2 = _ffn(x2, ada3, g_pre_f1[l], g_post_f1[l], w_f1_in[l], w_f1_out[l], mod_row=0, seq=seq)
        w_sp_pairs = w_spatial[l].reshape(H_A // 2, 2, CHUNK, CHUNK).transpose(0, 2, 1, 3)
        w_sp_pairs = w_sp_pairs.reshape(H_A // 2, CHUNK, 2 * CHUNK)
        b_sp_full = jnp.repeat(b_spatial[l].T, hd_a, axis=1)
        x2 = _mixer(x2, ada3, g_pre_m[l], g_post_m[l], w_mix_in[l].astype(_BF16), gmlp_norm_g[l],
                    gmlp_norm_b[l], w_sp_pairs, b_sp_full, conv_w[l], conv_b[l], conv_norm_g[l],
                    conv_norm_b[l], g_out_a[l], g_out_b[l], w_mix_out[l].astype(_BF16), seq=seq)
        x2 = _ffn(x2, ada3, g_pre_f2[l], g_post_f2[l], w_f2_in[l], w_f2_out[l], mod_row=6, seq=seq)
    return x2.reshape(bsz, seq, d)
```
